```python
import jax, jax.numpy as jnp
from jax import lax
import numpy as np

D_MODEL = 4096
BATCH = 2
SEQ = 8192
DEPTH = 1

F32 = jnp.float32
RMS_EPS = 1e-6

GLA_HEADS = 8
GLA_VAL_W = D_MODEL // 2
GLA_KEY_W = GLA_VAL_W // 2
GLA_DV = GLA_VAL_W // GLA_HEADS
GLA_DK = GLA_KEY_W // GLA_HEADS
GLA_GATE_RANK = 16
GLA_GATE_TAU = 16.0
GLA_CHUNK = 64

SWA_HEAD_DIM = 64
SWA_Q_HEADS = (D_MODEL // 2) // SWA_HEAD_DIM
SWA_KV_HEADS = SWA_Q_HEADS // 8
SWA_Q_W = SWA_Q_HEADS * SWA_HEAD_DIM
SWA_KV_W = SWA_KV_HEADS * SWA_HEAD_DIM
SWA_WINDOW = 128
SWA_BLOCK = 128
ROPE_THETA = 10000.0

MIX_WIDTH = GLA_VAL_W + SWA_Q_W
IN_SPLITS = (GLA_KEY_W, GLA_KEY_W, GLA_VAL_W, GLA_VAL_W, GLA_GATE_RANK,
             SWA_Q_W, SWA_KV_W, SWA_KV_W)
IN_COLS = GLA_KEY_W * 2 + GLA_VAL_W * 2 + GLA_GATE_RANK + SWA_Q_W + SWA_KV_W * 2

N_GROUPS = 8
EXPERTS_PER_GROUP = 8
N_EXPERTS = N_GROUPS * EXPERTS_PER_GROUP
TOP_K_IN_GROUP = 2
D_FF_EXPERT = D_MODEL // 8

kernel_name = 'hymba_gla_swa_hier_moe'


def rms_norm(x, g):
    xf = x.astype(F32)
    y = xf * lax.rsqrt(jnp.mean(xf * xf, axis=-1, keepdims=True) + RMS_EPS)
    return (y * g.astype(F32)).astype(x.dtype)


def split_points():
    pts, acc = [], 0
    for s in IN_SPLITS[:-1]:
        acc += s
        pts.append(acc)
    return pts


def gla_chunked(q, k, v, g_log):
    B, T, H, dk = q.shape
    dv = v.shape[-1]
    C = GLA_CHUNK
    N = T // C

    def chunked(a):
        return a.reshape(B, N, C, H, a.shape[-1]).transpose(0, 3, 1, 2, 4).astype(F32)

    qc = chunked(q) * (dk ** -0.5)
    kc, vc, gc = chunked(k), chunked(v), chunked(g_log)
    b = jnp.cumsum(gc, axis=3)
    b_last = b[:, :, :, -1:, :]
    q_in = qc * jnp.exp(b)
    k_in = kc * jnp.exp(-b)
    k_dec = kc * jnp.exp(b_last - b)

    causal = jnp.tril(jnp.ones((C, C), dtype=bool))
    a = jnp.einsum('bhnid,bhnjd->bhnij', q_in, k_in)
    a = jnp.where(causal, a, 0.0)
    o_intra = jnp.einsum('bhnij,bhnje->bhnie', a, vc)

    chunk_kv = jnp.einsum('bhncd,bhnce->bhnde', k_dec, vc)
    decay = jnp.exp(b_last[:, :, :, 0, :])

    def step(state, inp):
        kv_n, dec_n = inp
        return dec_n[..., None] * state + kv_n, state

    s0 = jnp.zeros((B, H, dk, dv), F32)
    _, s_start = lax.scan(step, s0, (jnp.moveaxis(chunk_kv, 2, 0), jnp.moveaxis(decay, 2, 0)))
    s_start = jnp.moveaxis(s_start, 0, 2)
    o_inter = jnp.einsum('bhncd,bhnde->bhnce', q_in, s_start)
    o = (o_intra + o_inter).transpose(0, 2, 3, 1, 4).reshape(B, T, H, dv)
    return o.astype(v.dtype)


def rope(x, positions):
    d = x.shape[-1]
    half = d // 2
    inv_freq = ROPE_THETA ** (-jnp.arange(half, dtype=F32) / half)
    ang = positions.astype(F32)[..., None] * inv_freq
    cos = jnp.cos(ang)[:, :, None, :]
    sin = jnp.sin(ang)[:, :, None, :]
    xf = x.astype(F32)
    x1, x2 = xf[..., :half], xf[..., half:]
    out = jnp.concatenate([x1 * cos - x2 * sin, x2 * cos + x1 * sin], axis=-1)
    return out.astype(x.dtype)


def sliding_window_attention(q, k, v, sinks):
    B, T, Hq, d = q.shape
    Hkv = k.shape[2]
    G = Hq // Hkv
    W = SWA_BLOCK
    N = T // W
    qb = q.reshape(B, N, W, Hkv, G, d).astype(F32)

    def band(a):
        ab = a.reshape(B, N, W, Hkv, d).astype(F32)
        prev = jnp.pad(ab, ((0, 0), (1, 0), (0, 0), (0, 0), (0, 0)))[:, :-1]
        return jnp.concatenate([prev, ab], axis=2)

    kb, vb = band(k), band(v)
    s = jnp.einsum('bnqhgd,bnkhd->bnhgqk', qb, kb) * (d ** -0.5)
    qi = jnp.arange(W)[:, None] + W
    kj = jnp.arange(2 * W)[None, :]
    diff = qi - kj
    in_window = (diff >= 0) & (diff < SWA_WINDOW)
    has_prev = (jnp.arange(N)[:, None, None] > 0) | (kj[None] >= W)
    valid = in_window[None] & has_prev
    s = jnp.where(valid[None, :, None, None], s, -jnp.inf)
    sink = jnp.broadcast_to(sinks.astype(F32).reshape(1, 1, Hkv, G, 1, 1), s.shape[:-1] + (1,))
    p = jax.nn.softmax(jnp.concatenate([s, sink], axis=-1), axis=-1)[..., :-1]
    o = jnp.einsum('bnhgqk,bnkhd->bnqhgd', p, vb)
    return o.reshape(B, T, Hq * d).astype(q.dtype)


def hierarchical_moe(h, rg_w, rg_b, re_w, re_b, w_gate, w_up, w_down):
    B, T, D = h.shape
    x = h.reshape(B * T, D)
    g_logits = (x @ rg_w).astype(F32) + rg_b.astype(F32)
    g_prob = jax.nn.softmax(g_logits, axis=-1)
    g_w, g_idx = lax.top_k(g_prob, 1)
    e_logits = jnp.einsum('nd,gde->nge', x, re_w).astype(F32) + re_b.astype(F32)
    e_sel = jnp.take_along_axis(e_logits, g_idx[:, :, None], axis=1)[:, 0]
    e_prob = jax.nn.softmax(e_sel, axis=-1)
    top_w, top_i = lax.top_k(e_prob, TOP_K_IN_GROUP)
    top_w = top_w / jnp.sum(top_w, axis=-1, keepdims=True)
    expert_id = g_idx * EXPERTS_PER_GROUP + top_i
    weights = g_w * top_w
    comb = jnp.sum(jax.nn.one_hot(expert_id, N_EXPERTS, dtype=F32) * weights[..., None], axis=1)

    def step(acc, inp):
        wg, wu, wd, c = inp
        hid = jax.nn.silu(x @ wg) * (x @ wu)
        return acc + c[:, None] * (hid @ wd).astype(F32), None

    acc, _ = lax.scan(step, jnp.zeros((B * T, D), F32), (w_gate, w_up, w_down, comb.T))
    return acc.astype(h.dtype).reshape(B, T, D)


def setup_inputs(seed: int = 0) -> dict:
    key = jax.random.key(seed)
    ks = jax.random.split(key, 20)
    L = DEPTH
    nrm = jax.random.normal
    x = nrm(ks[0], (BATCH, SEQ, D_MODEL), F32)
    offs = jax.random.randint(ks[1], (BATCH, 1), 0, 1024, dtype=jnp.int32)
    positions = (jnp.arange(SEQ, dtype=jnp.int32)[None, :] + offs).astype(jnp.int32)
    return {
        'x': x,
        'positions': positions,
        'norm_mix_g': 1.0 + 0.02 * nrm(ks[2], (L, D_MODEL), F32),
        'w_in': nrm(ks[3], (L, D_MODEL, IN_COLS), F32) * D_MODEL ** -0.5,
        'gla_gate_w2': nrm(ks[4], (L, GLA_GATE_RANK, GLA_KEY_W), F32) * GLA_GATE_RANK ** -0.5,
        'gla_gate_b': 0.1 * nrm(ks[5], (L, GLA_KEY_W), F32),
        'gla_out_norm_g': 1.0 + 0.02 * nrm(ks[6], (L, GLA_DV), F32),
        'swa_sinks': 0.5 * nrm(ks[7], (L, SWA_Q_HEADS), F32),
        'w_out': nrm(ks[8], (L, MIX_WIDTH, D_MODEL), F32) * MIX_WIDTH ** -0.5,
        'norm_ffn_g': 1.0 + 0.02 * nrm(ks[9], (L, D_MODEL), F32),
        'router_group_w': nrm(ks[10], (L, D_MODEL, N_GROUPS), F32) * D_MODEL ** -0.5,
        'router_group_b': 0.01 * nrm(ks[11], (L, N_GROUPS), F32),
        'router_expert_w': nrm(ks[12], (L, N_GROUPS, D_MODEL, EXPERTS_PER_GROUP), F32) * D_MODEL ** -0.5,
        'router_expert_b': 0.01 * nrm(ks[13], (L, N_GROUPS, EXPERTS_PER_GROUP), F32),
        'expert_w_gate': nrm(ks[14], (L, N_EXPERTS, D_MODEL, D_FF_EXPERT), F32) * D_MODEL ** -0.5,
        'expert_w_up': nrm(ks[15], (L, N_EXPERTS, D_MODEL, D_FF_EXPERT), F32) * D_MODEL ** -0.5,
        'expert_w_down': nrm(ks[16], (L, N_EXPERTS, D_FF_EXPERT, D_MODEL), F32) * D_FF_EXPERT ** -0.5,
        'norm_final_g': 1.0 + 0.02 * nrm(ks[17], (D_MODEL,), F32),
    }


def reference(x, positions, norm_mix_g, w_in, gla_gate_w2, gla_gate_b, gla_out_norm_g,
              swa_sinks, w_out, norm_ffn_g, router_group_w, router_group_b,
              router_expert_w, router_expert_b, expert_w_gate, expert_w_up,
              expert_w_down, norm_final_g):
    B, T, _ = x.shape
    h = x
    for l in range(DEPTH):
        u = rms_norm(h, norm_mix_g[l])
        proj = u @ w_in[l]
        gq, gk, gv, gr, ga, sq, sk, sv = jnp.split(proj, split_points(), axis=-1)

        g_log = jax.nn.log_sigmoid((ga @ gla_gate_w2[l]).astype(F32) + gla_gate_b[l].astype(F32)) / GLA_GATE_TAU
        o_gla = gla_chunked(gq.reshape(B, T, GLA_HEADS, GLA_DK),
                            gk.reshape(B, T, GLA_HEADS, GLA_DK),
                            gv.reshape(B, T, GLA_HEADS, GLA_DV),
                            g_log.reshape(B, T, GLA_HEADS, GLA_DK))
        o_gla = rms_norm(o_gla, gla_out_norm_g[l]) * jax.nn.silu(gr.reshape(B, T, GLA_HEADS, GLA_DV))
        o_gla = o_gla.reshape(B, T, GLA_VAL_W)

        q = rope(sq.reshape(B, T, SWA_Q_HEADS, SWA_HEAD_DIM), positions)
        k = rope(sk.reshape(B, T, SWA_KV_HEADS, SWA_HEAD_DIM), positions)
        v = sv.reshape(B, T, SWA_KV_HEADS, SWA_HEAD_DIM)
        o_swa = sliding_window_attention(q, k, v, swa_sinks[l])

        mixed = jnp.concatenate([o_gla.astype(h.dtype), o_swa.astype(h.dtype)], axis=-1)
        h = h + (mixed @ w_out[l]).astype(h.dtype)

        u = rms_norm(h, norm_ffn_g[l])
        h = h + hierarchical_moe(u, router_group_w[l], router_group_b[l],
                                 router_expert_w[l], router_expert_b[l],
                                 expert_w_gate[l], expert_w_up[l], expert_w_down[l])
    return rms_norm(h, norm_final_g)
```

```python
import functools

import jax
import jax.numpy as jnp
from jax import lax
from jax.experimental import pallas as pl
from jax.experimental.pallas import tpu as pltpu

F32 = jnp.float32
BF16 = jnp.bfloat16
I32 = jnp.int32

RMS_EPS = 1e-6

GLA_HEADS = 8
GLA_GATE_TAU = 16.0
GLA_CHUNK = 64
SWA_HEAD_DIM = 64
SWA_GROUP = 8
SWA_BLOCK = 128
ROPE_THETA = 10000.0
N_GROUPS = 8
EXPERTS_PER_GROUP = 8
TOP_K = 2

LANES = 128
VMEM_LIMIT_BYTES = 56 * 1024 * 1024

NEG_BIG = -1e30


def _cparams(semantics):
    return pltpu.CompilerParams(dimension_semantics=semantics,
                                vmem_limit_bytes=VMEM_LIMIT_BYTES)


def _norm_gate_kernel(x_ref, g_ref, wga_ref, u_ref, ga_ref):
    x = x_ref[...]
    ms = jnp.mean(x * x, axis=-1, keepdims=True)
    u = (x * lax.rsqrt(ms + RMS_EPS) * g_ref[...]).astype(BF16)
    u_ref[...] = u
    ga_ref[...] = jnp.dot(u, wga_ref[...], preferred_element_type=F32)


def _norm_gate(x2d, g, wga, *, rows):
    n, d = x2d.shape
    return pl.pallas_call(
        _norm_gate_kernel,
        grid=(n // rows,),
        in_specs=[pl.BlockSpec((rows, d), lambda i: (i, 0)),
                  pl.BlockSpec((1, d), lambda i: (0, 0)),
                  pl.BlockSpec((d, LANES), lambda i: (0, 0))],
        out_specs=[pl.BlockSpec((rows, d), lambda i: (i, 0)),
                   pl.BlockSpec((rows, LANES), lambda i: (i, 0))],
        out_shape=[jax.ShapeDtypeStruct((n, d), BF16),
                   jax.ShapeDtypeStruct((n, LANES), F32)],
        compiler_params=_cparams(("parallel",)),
        name="norm_gate",
    )(x2d, g, wga)


def _matmul_kernel(a_ref, w_ref, o_ref):
    o_ref[...] = jnp.dot(a_ref[...], w_ref[...],
                         preferred_element_type=F32).astype(o_ref.dtype)


def _matmul(a, w, *, tm, tn, out_dtype, name):
    m, k = a.shape
    _, n = w.shape
    return pl.pallas_call(
        _matmul_kernel,
        grid=(m // tm, n // tn),
        in_specs=[pl.BlockSpec((tm, k), lambda i, j: (i, 0)),
                  pl.BlockSpec((k, tn), lambda i, j: (0, j))],
        out_specs=pl.BlockSpec((tm, tn), lambda i, j: (i, j)),
        out_shape=jax.ShapeDtypeStruct((m, n), out_dtype),
        compiler_params=_cparams(("parallel", "arbitrary")),
        name=name,
    )(a, w)


def _out_proj_kernel(a_ref, b_ref, wa_ref, wb_ref, x_ref, o_ref):
    acc = jnp.dot(a_ref[...], wa_ref[...], preferred_element_type=F32)
    acc = acc + jnp.dot(b_ref[...], wb_ref[...], preferred_element_type=F32)
    o_ref[...] = x_ref[...] + acc


def _out_proj(a, b, w, x2d, *, tm, tn):
    m, ka = a.shape
    _, kb = b.shape
    _, n = w.shape
    return pl.pallas_call(
        _out_proj_kernel,
        grid=(m // tm, n // tn),
        in_specs=[pl.BlockSpec((tm, ka), lambda i, j: (i, 0)),
                  pl.BlockSpec((tm, kb), lambda i, j: (i, 0)),
                  pl.BlockSpec((ka, tn), lambda i, j: (0, j)),
                  pl.BlockSpec((kb, tn), lambda i, j: (ka // kb, j)),
                  pl.BlockSpec((tm, tn), lambda i, j: (i, j))],
        out_specs=pl.BlockSpec((tm, tn), lambda i, j: (i, j)),
        out_shape=jax.ShapeDtypeStruct((m, n), F32),
        compiler_params=_cparams(("parallel", "arbitrary")),
        name="out_proj",
    )(a, b, w, w, x2d)


def _gla_kernel(q_ref, k_ref, v_ref, r_ref, ga_ref, w2_ref, gb_ref, gn_ref,
                o_ref, st_ref, *, chunk, dk, dv):
    @pl.when(pl.program_id(2) == 0)
    def _():
        st_ref[...] = jnp.zeros_like(st_ref)

    tc = q_ref.shape[1]
    row = lax.broadcasted_iota(I32, (chunk, chunk), 0)
    col = lax.broadcasted_iota(I32, (chunk, chunk), 1)
    causal = row >= col
    tri = causal.astype(BF16)
    scale = dk ** -0.5
    st = st_ref[...]
    for c in range(tc // chunk):
        sl = pl.ds(c * chunk, chunk)
        q = q_ref[0, sl, :].astype(F32) * scale
        k = k_ref[0, sl, :].astype(F32)
        v = v_ref[0, sl, :]
        z = jnp.dot(ga_ref[0, sl, :].astype(BF16), w2_ref[...],
                    preferred_element_type=F32) + gb_ref[...]
        g = -(jnp.maximum(-z, 0.0) + jnp.log1p(jnp.exp(-jnp.abs(z)))) / GLA_GATE_TAU
        g_hi = g.astype(BF16)
        g_lo = (g - g_hi.astype(F32)).astype(BF16)
        b = (jnp.dot(tri, g_hi, preferred_element_type=F32)
             + jnp.dot(tri, g_lo, preferred_element_type=F32))
        b_last = b[chunk - 1:chunk, :]
        q_in = (q * jnp.exp(b)).astype(BF16)
        k_in = (k * jnp.exp(-b)).astype(BF16)
        k_dec = (k * jnp.exp(b_last - b)).astype(BF16)
        a = lax.dot_general(q_in, k_in, (((1,), (1,)), ((), ())),
                            preferred_element_type=F32)
        a = jnp.where(causal, a, 0.0).astype(BF16)
        o = jnp.dot(a, v, preferred_element_type=F32)
        o = o + lax.dot_general(q_in, st.astype(BF16), (((1,), (1,)), ((), ())),
                                preferred_element_type=F32)
        kv_t = lax.dot_general(v, k_dec, (((0,), (0,)), ((), ())),
                               preferred_element_type=F32)
        st = st * jnp.exp(b_last) + kv_t
        ms = jnp.mean(o * o, axis=-1, keepdims=True)
        o = o * lax.rsqrt(ms + RMS_EPS) * gn_ref[...]
        r = r_ref[0, sl, :].astype(F32)
        o = o * (r / (1.0 + jnp.exp(-r)))
        o_ref[0, sl, :] = o.astype(o_ref.dtype)
    st_ref[...] = st


def _gla(proj, ga, w2p, gb, gn, *, heads, dk, dv, tc, col_q, col_k, col_v, col_r):
    bsz, t, _ = proj.shape
    kern = functools.partial(_gla_kernel, chunk=GLA_CHUNK, dk=dk, dv=dv)
    return pl.pallas_call(
        kern,
        grid=(bsz, heads, t // tc),
        in_specs=[
            pl.BlockSpec((1, tc, dk), lambda b, h, i: (b, i, col_q // dk + h)),
            pl.BlockSpec((1, tc, dk), lambda b, h, i: (b, i, col_k // dk + h)),
            pl.BlockSpec((1, tc, dv), lambda b, h, i: (b, i, col_v // dv + h)),
            pl.BlockSpec((1, tc, dv), lambda b, h, i: (b, i, col_r // dv + h)),
            pl.BlockSpec((1, tc, LANES), lambda b, h, i: (b, i, 0)),
            pl.BlockSpec((LANES, dk), lambda b, h, i: (0, h)),
            pl.BlockSpec((1, dk), lambda b, h, i: (0, h)),
            pl.BlockSpec((1, dv), lambda b, h, i: (0, 0)),
        ],
        out_specs=pl.BlockSpec((1, tc, dv), lambda b, h, i: (b, i, h)),
        out_shape=jax.ShapeDtypeStruct((bsz, t, heads * dv), BF16),
        scratch_shapes=[pltpu.VMEM((dv, dk), F32)],
        compiler_params=_cparams(("parallel", "parallel", "arbitrary")),
        name="gla",
    )(proj, proj, proj, proj, ga, w2p, gb, gn)


def _rope_table_kernel(pos_ref, freq_ref, cos_ref, sin_ref):
    ang = pos_ref[...] * freq_ref[...]
    cos_ref[...] = jnp.cos(ang)
    sin_ref[...] = jnp.sin(ang)


def _rope_tables(posf, freq, *, rows):
    n = posf.shape[0]
    spec = pl.BlockSpec((rows, LANES), lambda i: (i, 0))
    return pl.pallas_call(
        _rope_table_kernel,
        grid=(n // rows,),
        in_specs=[spec, pl.BlockSpec((1, LANES), lambda i: (0, 0))],
        out_specs=[spec, spec],
        out_shape=[jax.ShapeDtypeStruct((n, LANES), F32)] * 2,
        compiler_params=_cparams(("parallel",)),
        name="rope_tables",
    )(posf, freq)


def _rotate_half(x, first_half):
    half = SWA_HEAD_DIM // 2
    width = x.shape[-1]
    fwd = pltpu.roll(x, width - half, 1)
    bwd = pltpu.roll(x, half, 1)
    return jnp.where(first_half, -fwd, bwd)


def _swa_kernel(q_ref, kp_ref, kc_ref, vp_ref, vc_ref, cp_ref, sp_ref, cc_ref,
                sc_ref, sink_ref, o_ref, *, kv_heads):
    w = SWA_BLOCK
    d = SWA_HEAD_DIM
    blk = pl.program_id(1)
    lane = lax.broadcasted_iota(I32, (w, LANES), 1)
    first_half = (lane % d) < (d // 2)
    low_head = lane < d

    def rope(x, cos, sin, fh):
        return x * cos + _rotate_half(x, fh) * sin

    cos_c, sin_c = cc_ref[...], sc_ref[...]
    cos_p, sin_p = cp_ref[...], sp_ref[...]

    kw = kv_heads * d
    lane_k = lax.broadcasted_iota(I32, (w, kw), 1)
    fh_k = (lane_k % d) < (d // 2)
    reps = kw // LANES
    tile = lambda t: jnp.concatenate([t] * reps, axis=1) if reps > 1 else t
    k_prev = rope(kp_ref[0].astype(F32), tile(cos_p), tile(sin_p), fh_k)
    k_cur = rope(kc_ref[0].astype(F32), tile(cos_c), tile(sin_c), fh_k)
    k_all = jnp.concatenate([k_prev, k_cur], axis=0)
    v_all = jnp.concatenate([vp_ref[0], vc_ref[0]], axis=0)

    qi = lax.broadcasted_iota(I32, (w, 2 * w), 0) + w
    kj = lax.broadcasted_iota(I32, (w, 2 * w), 1)
    diff = qi - kj
    valid = (diff >= 0) & (diff < w) & ((kj >= w) | (blk > 0))

    scale = d ** -0.5
    pairs_per_kv = SWA_GROUP // 2
    for g in range(kv_heads):
        kg = k_all[:, g * d:(g + 1) * d]
        vg = v_all[:, g * d:(g + 1) * d]
        zk = jnp.zeros_like(kg)
        zv = jnp.zeros_like(vg)
        k_lo = jnp.concatenate([kg, zk], axis=1).astype(BF16)
        k_hi = jnp.concatenate([zk, kg], axis=1).astype(BF16)
        v_lo = jnp.concatenate([vg, zv], axis=1)
        v_hi = jnp.concatenate([zv, vg], axis=1)
        for p in range(pairs_per_kv):
            cb = g * pairs_per_kv + p
            q2 = q_ref[0, :, cb * LANES:(cb + 1) * LANES].astype(F32)
            q2 = (rope(q2, cos_c, sin_c, first_half) * scale).astype(BF16)
            sink2 = sink_ref[:, cb * LANES:(cb + 1) * LANES]
            acc = None
            for kk, vv, sel in ((k_lo, v_lo, low_head), (k_hi, v_hi, ~low_head)):
                s = lax.dot_general(q2, kk, (((1,), (1,)), ((), ())),
                                    preferred_element_type=F32)
                s = jnp.where(valid, s, NEG_BIG)
                sink = jnp.max(jnp.where(sel[:1], sink2, NEG_BIG), axis=-1, keepdims=True)
                m = jnp.maximum(jnp.max(s, axis=-1, keepdims=True), sink)
                p_un = jnp.exp(s - m)
                denom = jnp.sum(p_un, axis=-1, keepdims=True) + jnp.exp(sink - m)
                o = jnp.dot(p_un.astype(BF16), vv, preferred_element_type=F32)
                o = o / denom
                acc = o if acc is None else acc + o
            o_ref[0, :, cb * LANES:(cb + 1) * LANES] = acc.astype(o_ref.dtype)


def _swa(proj, cos, sin, sinks_lane, *, q_heads, kv_heads, col_q, col_k, col_v):
    bsz, t, _ = proj.shape
    w = SWA_BLOCK
    qw = q_heads * SWA_HEAD_DIM
    kw = kv_heads * SWA_HEAD_DIM
    prev = lambda i: jnp.maximum(i - 1, 0)
    kern = functools.partial(_swa_kernel, kv_heads=kv_heads)
    tab_p = pl.BlockSpec((w, LANES), lambda b, i: (b * (t // w) + prev(i), 0))
    tab_c = pl.BlockSpec((w, LANES), lambda b, i: (b * (t // w) + i, 0))
    return pl.pallas_call(
        kern,
        grid=(bsz, t // w),
        in_specs=[
            pl.BlockSpec((1, w, qw), lambda b, i: (b, i, col_q // qw)),
            pl.BlockSpec((1, w, kw), lambda b, i: (b, prev(i), col_k // kw)),
            pl.BlockSpec((1, w, kw), lambda b, i: (b, i, col_k // kw)),
            pl.BlockSpec((1, w, kw), lambda b, i: (b, prev(i), col_v // kw)),
            pl.BlockSpec((1, w, kw), lambda b, i: (b, i, col_v // kw)),
            tab_p, tab_p, tab_c, tab_c,
            pl.BlockSpec((1, qw), lambda b, i: (0, 0)),
        ],
        out_specs=pl.BlockSpec((1, w, qw), lambda b, i: (b, i, 0)),
        out_shape=jax.ShapeDtypeStruct((bsz, t, qw), BF16),
        compiler_params=_cparams(("parallel", "arbitrary")),
        name="swa",
    )(proj, proj, proj, proj, proj, cos, sin, cos, sin, sinks_lane)


def _router_kernel(h_ref, g_ref, wr_ref, br_ref, u_ref, rt_ref):
    h = h_ref[...]
    ms = jnp.mean(h * h, axis=-1, keepdims=True)
    u = h * lax.rsqrt(ms + RMS_EPS) * g_ref[...]
    u_ref[...] = u.astype(BF16)
    logits = jnp.dot(u, wr_ref[...], preferred_element_type=F32,
                     precision=lax.Precision.HIGHEST) + br_ref[...]
    lane = lax.broadcasted_iota(I32, logits.shape, 1)
    is_group = lane < N_GROUPS
    gl = jnp.where(is_group, logits, NEG_BIG)
    gmax = jnp.max(gl, axis=-1, keepdims=True)
    gidx = jnp.min(jnp.where(gl == gmax, lane, LANES), axis=-1, keepdims=True)
    gsum = jnp.sum(jnp.where(is_group, jnp.exp(gl - gmax), 0.0), axis=-1, keepdims=True)
    g_w = 1.0 / gsum
    lo = N_GROUPS + gidx * EXPERTS_PER_GROUP
    sel = (lane >= lo) & (lane < lo + EXPERTS_PER_GROUP)
    el = jnp.where(sel, logits, NEG_BIG)
    m1 = jnp.max(el, axis=-1, keepdims=True)
    i1 = jnp.min(jnp.where(el == m1, lane, LANES), axis=-1, keepdims=True)
    el2 = jnp.where(lane == i1, NEG_BIG, el)
    m2 = jnp.max(el2, axis=-1, keepdims=True)
    i2 = jnp.min(jnp.where(el2 == m2, lane, LANES), axis=-1, keepdims=True)
    t = jnp.exp(m2 - m1)
    w1 = g_w / (1.0 + t)
    w2 = g_w * t / (1.0 + t)
    e1 = (i1 - N_GROUPS).astype(F32)
    e2 = (i2 - N_GROUPS).astype(F32)
    rt_ref[...] = jnp.where(lane == 0, e1,
                  jnp.where(lane == 1, e2,
                  jnp.where(lane == 2, w1,
                  jnp.where(lane == 3, w2, 0.0))))


def _router(h2d, g, wr, br, *, rows):
    n, d = h2d.shape
    return pl.pallas_call(
        _router_kernel,
        grid=(n // rows,),
        in_specs=[pl.BlockSpec((rows, d), lambda i: (i, 0)),
                  pl.BlockSpec((1, d), lambda i: (0, 0)),
                  pl.BlockSpec((d, LANES), lambda i: (0, 0)),
                  pl.BlockSpec((1, LANES), lambda i: (0, 0))],
        out_specs=[pl.BlockSpec((rows, d), lambda i: (i, 0)),
                   pl.BlockSpec((rows, LANES), lambda i: (i, 0))],
        out_shape=[jax.ShapeDtypeStruct((n, d), BF16),
                   jax.ShapeDtypeStruct((n, LANES), F32)],
        compiler_params=_cparams(("parallel",)),
        name="router",
    )(h2d, g, wr, br)


def _row_copy_kernel(nu_ref, idx_ref, src_ref, dst_ref, sem, *, tm, indexed_src):
    i = pl.program_id(0)
    n_used = nu_ref[0]
    base = i * tm

    def tile_wait(slot):
        pltpu.make_async_copy(dst_ref.at[pl.ds(0, tm)], dst_ref.at[pl.ds(0, tm)],
                              sem.at[slot]).wait()

    @pl.when(i < n_used)
    def _():
        slot = i % 2

        def issue(k, carry):
            r = idx_ref[0, 0, k]
            if indexed_src:
                cp = pltpu.make_async_copy(src_ref.at[r], dst_ref.at[base + k], sem.at[slot])
            else:
                cp = pltpu.make_async_copy(src_ref.at[base + k], dst_ref.at[r], sem.at[slot])
            cp.start()
            return carry

        lax.fori_loop(0, tm, issue, 0, unroll=8)

        @pl.when(i > 0)
        def _():
            tile_wait(1 - slot)

        @pl.when(i == n_used - 1)
        def _():
            tile_wait(slot)


def _row_copy(n_used, idx, src, n_dst_rows, *, tm, indexed_src, name):
    n_tiles = idx.shape[0]
    kern = functools.partial(_row_copy_kernel, tm=tm, indexed_src=indexed_src)
    grid_spec = pltpu.PrefetchScalarGridSpec(
        num_scalar_prefetch=1,
        grid=(n_tiles,),
        in_specs=[pl.BlockSpec((1, 1, tm), lambda i, nu: (i, 0, 0),
                               memory_space=pltpu.SMEM),
                  pl.BlockSpec(memory_space=pl.ANY)],
        out_specs=pl.BlockSpec(memory_space=pl.ANY),
        scratch_shapes=[pltpu.SemaphoreType.DMA((2,))],
    )
    return pl.pallas_call(
        kern,
        grid_spec=grid_spec,
        out_shape=jax.ShapeDtypeStruct((n_dst_rows,) + src.shape[1:], src.dtype),
        compiler_params=_cparams(("arbitrary",)),
        name=name,
    )(n_used, idx, src)


def _expert_kernel(te_ref, nu_ref, x_ref, wg_ref, wu_ref, wd_ref, y_ref):
    @pl.when(pl.program_id(0) < nu_ref[0])
    def _():
        x = x_ref[...]
        a = jnp.dot(x, wg_ref[0], preferred_element_type=F32)
        b = jnp.dot(x, wu_ref[0], preferred_element_type=F32)
        hid = (a / (1.0 + jnp.exp(-a)) * b).astype(BF16)
        y_ref[...] = jnp.dot(hid, wd_ref[0], preferred_element_type=F32).astype(y_ref.dtype)


def _experts(tile_expert, n_used, xs, wg, wu, wd, *, tm):
    p, d = xs.shape
    _, _, f = wg.shape
    n_tiles = p // tm
    row_map = lambda i, te, nu: (jnp.minimum(i, nu[0] - 1), 0)
    grid_spec = pltpu.PrefetchScalarGridSpec(
        num_scalar_prefetch=2,
        grid=(n_tiles,),
        in_specs=[pl.BlockSpec((tm, d), row_map),
                  pl.BlockSpec((1, d, f), lambda i, te, nu: (te[i], 0, 0)),
                  pl.BlockSpec((1, d, f), lambda i, te, nu: (te[i], 0, 0)),
                  pl.BlockSpec((1, f, d), lambda i, te, nu: (te[i], 0, 0))],
        out_specs=pl.BlockSpec((tm, d), row_map),
    )
    return pl.pallas_call(
        _expert_kernel,
        grid_spec=grid_spec,
        out_shape=jax.ShapeDtypeStruct((p, d), BF16),
        compiler_params=_cparams(("arbitrary",)),
        name="experts",
    )(tile_expert, n_used, xs, wg, wu, wd)


def _combine_kernel(h_ref, y0_ref, y1_ref, rt_ref, g_ref, o_ref, *, final_norm):
    rt = rt_ref[...]
    w0 = rt[:, 2:3]
    w1 = rt[:, 3:4]
    h = h_ref[...] + w0 * y0_ref[...].astype(F32) + w1 * y1_ref[...].astype(F32)
    if final_norm:
        ms = jnp.mean(h * h, axis=-1, keepdims=True)
        h = h * lax.rsqrt(ms + RMS_EPS) * g_ref[...]
    o_ref[...] = h


def _combine(h2d, y, route, g, *, rows, final_norm):
    n, d = h2d.shape
    nb = n // rows
    return pl.pallas_call(
        functools.partial(_combine_kernel, final_norm=final_norm),
        grid=(nb,),
        in_specs=[pl.BlockSpec((rows, d), lambda i: (i, 0)),
                  pl.BlockSpec((rows, d), lambda i: (i, 0)),
                  pl.BlockSpec((rows, d), lambda i: (nb + i, 0)),
                  pl.BlockSpec((rows, LANES), lambda i: (i, 0)),
                  pl.BlockSpec((1, d), lambda i: (0, 0))],
        out_specs=pl.BlockSpec((rows, d), lambda i: (i, 0)),
        out_shape=jax.ShapeDtypeStruct((n, d), F32),
        compiler_params=_cparams(("parallel",)),
        name="combine",
    )(h2d, y, y, route, g)


def _dispatch_plan(eid, n_experts, tm, p_rows):
    n = eid.shape[0]
    n_flat = TOP_K * n
    n_tiles = p_rows // tm
    e_flat = eid.T.reshape(-1)
    onehot = (e_flat[:, None] == jnp.arange(n_experts, dtype=I32)[None, :]).astype(I32)
    csum = jnp.cumsum(onehot, axis=0)
    counts = csum[-1]
    rank = jnp.take_along_axis(csum, e_flat[:, None], axis=1)[:, 0] - 1
    pcounts = ((counts + tm - 1) // tm) * tm
    pend = jnp.cumsum(pcounts)
    pstart = pend - pcounts
    dest = pstart[e_flat] + rank
    n_used = (pend[-1] // tm).astype(I32)
    inv = jnp.full((p_rows,), n_flat, I32).at[dest].set(jnp.arange(n_flat, dtype=I32))
    is_pad = inv >= n_flat
    src_tok = jnp.where(is_pad, 0, inv % n)
    pad_slot = n_flat + jnp.cumsum(is_pad.astype(I32)) - 1
    out_row = jnp.where(is_pad, pad_slot, inv)
    tile_start = jnp.arange(n_tiles, dtype=I32) * tm
    tile_expert = jnp.searchsorted(pend, tile_start, side="right").astype(I32)
    last_expert = tile_expert[jnp.maximum(n_used - 1, 0)]
    tile_expert = jnp.where(tile_start < pend[-1], tile_expert, last_expert)
    tile_expert = jnp.minimum(tile_expert, n_experts - 1)
    return (src_tok.reshape(n_tiles, 1, tm), out_row.reshape(n_tiles, 1, tm),
            tile_expert, n_used.reshape(1))


def _tiles(n_tokens, d_model):
    return dict(
        norm_rows=256,
        mm_tm=1024, mm_tn=512,
        gla_tc=512,
        moe_tm=256,
    )


def kernel(x, positions, norm_mix_g, w_in, gla_gate_w2, gla_gate_b, gla_out_norm_g,
           swa_sinks, w_out, norm_ffn_g, router_group_w, router_group_b,
           router_expert_w, router_expert_b, expert_w_gate, expert_w_up,
           expert_w_down, norm_final_g):
    bsz, t, d = x.shape
    n = bsz * t
    depth = w_in.shape[0]
    cfg = _tiles(n, d)

    rank = gla_gate_w2.shape[1]
    key_w = gla_gate_w2.shape[2]
    val_w = GLA_HEADS * gla_out_norm_g.shape[1]
    dk, dv = key_w // GLA_HEADS, val_w // GLA_HEADS
    q_heads = swa_sinks.shape[1]
    kv_heads = q_heads // SWA_GROUP
    sq_w, skv_w = q_heads * SWA_HEAD_DIM, kv_heads * SWA_HEAD_DIM
    n_experts = expert_w_gate.shape[1]
    tm = cfg["moe_tm"]
    p_rows = TOP_K * n + n_experts * tm

    c_gq, c_gk, c_gv, c_gr = 0, key_w, 2 * key_w, 2 * key_w + val_w
    c_ga = 2 * key_w + 2 * val_w
    c_sq = c_ga + rank
    m_sq = c_ga
    m_sk, m_sv = m_sq + sq_w, m_sq + sq_w + skv_w

    half = SWA_HEAD_DIM // 2
    inv_freq = ROPE_THETA ** (-jnp.arange(half, dtype=F32) / half)
    freq_lane = jnp.tile(inv_freq, LANES // half).reshape(1, LANES)
    posf = jnp.broadcast_to(positions.astype(F32).reshape(n, 1), (n, LANES))
    cos_t, sin_t = _rope_tables(posf, freq_lane, rows=1024)

    h2d = x.reshape(n, d)
    for l in range(depth):
        w_l = w_in[l]
        w_main = jnp.concatenate([w_l[:, :c_ga], w_l[:, c_sq:]], axis=1).astype(BF16)
        w_ga = jnp.pad(w_l[:, c_ga:c_sq], ((0, 0), (0, LANES - rank))).astype(BF16)
        w2p = jnp.pad(gla_gate_w2[l], ((0, LANES - rank), (0, 0))).astype(BF16)

        u, ga = _norm_gate(h2d, norm_mix_g[l].reshape(1, d), w_ga, rows=cfg["norm_rows"])
        proj = _matmul(u, w_main, tm=cfg["mm_tm"], tn=cfg["mm_tn"], out_dtype=BF16,
                       name="in_proj")
        proj3 = proj.reshape(bsz, t, -1)

        o_gla = _gla(proj3, ga.reshape(bsz, t, LANES), w2p,
                     gla_gate_b[l].reshape(1, key_w), gla_out_norm_g[l].reshape(1, dv),
                     heads=GLA_HEADS, dk=dk, dv=dv, tc=cfg["gla_tc"],
                     col_q=c_gq, col_k=c_gk, col_v=c_gv, col_r=c_gr)
        sinks_lane = jnp.repeat(swa_sinks[l], SWA_HEAD_DIM).reshape(1, sq_w)
        o_swa = _swa(proj3, cos_t, sin_t, sinks_lane, q_heads=q_heads, kv_heads=kv_heads,
                     col_q=m_sq, col_k=m_sk, col_v=m_sv)

        h2d = _out_proj(o_gla.reshape(n, val_w), o_swa.reshape(n, sq_w),
                        w_out[l].astype(BF16), h2d, tm=cfg["mm_tm"], tn=cfg["mm_tn"])

        wr = jnp.concatenate(
            [router_group_w[l],
             jnp.transpose(router_expert_w[l], (1, 0, 2)).reshape(d, n_experts)], axis=1)
        n_router = wr.shape[1]
        wr = jnp.pad(wr, ((0, 0), (0, LANES - n_router)))
        br = jnp.pad(jnp.concatenate([router_group_b[l], router_expert_b[l].reshape(-1)]),
                     (0, LANES - n_router)).reshape(1, LANES)
        u2, route = _router(h2d, norm_ffn_g[l].reshape(1, d), wr, br, rows=cfg["norm_rows"])

        eid = route[:, :TOP_K].astype(I32)
        src_tok, out_row, tile_expert, n_used = _dispatch_plan(eid, n_experts, tm, p_rows)

        slabs = d // LANES
        xs = _row_copy(n_used, src_tok, u2.reshape(n, slabs, LANES), p_rows,
                       tm=tm, indexed_src=True, name="moe_gather")
        ys = _experts(tile_expert, n_used, xs.reshape(p_rows, d),
                      expert_w_gate[l].astype(BF16), expert_w_up[l].astype(BF16),
                      expert_w_down[l].astype(BF16), tm=tm)
        y_tok = _row_copy(n_used, out_row, ys.reshape(p_rows, slabs, LANES), p_rows,
                          tm=tm, indexed_src=False, name="moe_scatter")

        h2d = _combine(h2d, y_tok.reshape(p_rows, d), route, norm_final_g.reshape(1, d),
                       rows=cfg["norm_rows"], final_norm=(l == depth - 1))
    return h2d.reshape(bsz, t, d)
```

```python
import functools

import jax
import jax.numpy as jnp
from jax import lax
from jax.experimental import pallas as pl
from jax.experimental.pallas import tpu as pltpu

F32 = jnp.float32
BF16 = jnp.bfloat16
I32 = jnp.int32

RMS_EPS = 1e-6

GLA_HEADS = 8
GLA_GATE_TAU = 16.0
GLA_CHUNK = 64
SWA_HEAD_DIM = 64
SWA_GROUP = 8
SWA_BLOCK = 128
ROPE_THETA = 10000.0
N_GROUPS = 8
EXPERTS_PER_GROUP = 8
TOP_K = 2

LANES = 128
VMEM_LIMIT_BYTES = 56 * 1024 * 1024

NEG_BIG = -1e30


def _cparams(semantics):
    return pltpu.CompilerParams(dimension_semantics=semantics,
                                vmem_limit_bytes=VMEM_LIMIT_BYTES)


def _norm_gate_kernel(x_ref, g_ref, wga_ref, u_ref, ga_ref):
    x = x_ref[...]
    ms = jnp.mean(x * x, axis=-1, keepdims=True)
    u = (x * lax.rsqrt(ms + RMS_EPS) * g_ref[...]).astype(BF16)
    u_ref[...] = u
    ga_ref[...] = jnp.dot(u, wga_ref[...], preferred_element_type=F32)


def _norm_gate(x2d, g, wga, *, rows):
    n, d = x2d.shape
    return pl.pallas_call(
        _norm_gate_kernel,
        grid=(n // rows,),
        in_specs=[pl.BlockSpec((rows, d), lambda i: (i, 0)),
                  pl.BlockSpec((1, d), lambda i: (0, 0)),
                  pl.BlockSpec((d, LANES), lambda i: (0, 0))],
        out_specs=[pl.BlockSpec((rows, d), lambda i: (i, 0)),
                   pl.BlockSpec((rows, LANES), lambda i: (i, 0))],
        out_shape=[jax.ShapeDtypeStruct((n, d), BF16),
                   jax.ShapeDtypeStruct((n, LANES), F32)],
        compiler_params=_cparams(("parallel",)),
        name="norm_gate",
    )(x2d, g, wga)


def _matmul_kernel(a_ref, w_ref, o_ref):
    o_ref[...] = jnp.dot(a_ref[...], w_ref[...],
                         preferred_element_type=F32).astype(o_ref.dtype)


def _matmul(a, w, *, tm, tn, out_dtype, name):
    m, k = a.shape
    _, n = w.shape
    return pl.pallas_call(
        _matmul_kernel,
        grid=(m // tm, n // tn),
        in_specs=[pl.BlockSpec((tm, k), lambda i, j: (i, 0)),
                  pl.BlockSpec((k, tn), lambda i, j: (0, j))],
        out_specs=pl.BlockSpec((tm, tn), lambda i, j: (i, j)),
        out_shape=jax.ShapeDtypeStruct((m, n), out_dtype),
        compiler_params=_cparams(("parallel", "arbitrary")),
        name=name,
    )(a, w)


def _out_proj_kernel(a_ref, b_ref, wa_ref, wb_ref, x_ref, o_ref):
    acc = jnp.dot(a_ref[...], wa_ref[...], preferred_element_type=F32)
    acc = acc + jnp.dot(b_ref[...], wb_ref[...], preferred_element_type=F32)
    o_ref[...] = x_ref[...] + acc


def _out_proj(a, b, w, x2d, *, tm, tn):
    m, ka = a.shape
    _, kb = b.shape
    _, n = w.shape
    return pl.pallas_call(
        _out_proj_kernel,
        grid=(m // tm, n // tn),
        in_specs=[pl.BlockSpec((tm, ka), lambda i, j: (i, 0)),
                  pl.BlockSpec((tm, kb), lambda i, j: (i, 0)),
                  pl.BlockSpec((ka, tn), lambda i, j: (0, j)),
                  pl.BlockSpec((kb, tn), lambda i, j: (ka // kb, j)),
                  pl.BlockSpec((tm, tn), lambda i, j: (i, j))],
        out_specs=pl.BlockSpec((tm, tn), lambda i, j: (i, j)),
        out_shape=jax.ShapeDtypeStruct((m, n), F32),
        compiler_params=_cparams(("parallel", "arbitrary")),
        name="out_proj",
    )(a, b, w, w, x2d)


def _gla_kernel(q_ref, k_ref, v_ref, r_ref, ga_ref, w2_ref, gb_ref, gn_ref,
                o_ref, st_ref, *, chunk, dk, dv):
    @pl.when(pl.program_id(2) == 0)
    def _():
        st_ref[...] = jnp.zeros_like(st_ref)

    tc = q_ref.shape[1]
    row = lax.broadcasted_iota(I32, (chunk, chunk), 0)
    col = lax.broadcasted_iota(I32, (chunk, chunk), 1)
    causal = row >= col
    tri = causal.astype(BF16)
    scale = dk ** -0.5
    st = st_ref[...]
    for c in range(tc // chunk):
        sl = pl.ds(c * chunk, chunk)
        q = q_ref[0, sl, :].astype(F32) * scale
        k = k_ref[0, sl, :].astype(F32)
        v = v_ref[0, sl, :]
        z = jnp.dot(ga_ref[0, sl, :].astype(BF16), w2_ref[...],
                    preferred_element_type=F32) + gb_ref[...]
        g = -(jnp.maximum(-z, 0.0) + jnp.log1p(jnp.exp(-jnp.abs(z)))) / GLA_GATE_TAU
        g_hi = g.astype(BF16)
        g_lo = (g - g_hi.astype(F32)).astype(BF16)
        b = (jnp.dot(tri, g_hi, preferred_element_type=F32)
             + jnp.dot(tri, g_lo, preferred_element_type=F32))
        b_last = b[chunk - 1:chunk, :]
        q_in = (q * jnp.exp(b)).astype(BF16)
        k_in = (k * jnp.exp(-b)).astype(BF16)
        k_dec = (k * jnp.exp(b_last - b)).astype(BF16)
        a = lax.dot_general(q_in, k_in, (((1,), (1,)), ((), ())),
                            preferred_element_type=F32)
        a = jnp.where(causal, a, 0.0).astype(BF16)
        o = jnp.dot(a, v, preferred_element_type=F32)
        o = o + lax.dot_general(q_in, st.astype(BF16), (((1,), (1,)), ((), ())),
                                preferred_element_type=F32)
        kv_t = lax.dot_general(v, k_dec, (((0,), (0,)), ((), ())),
                               preferred_element_type=F32)
        st = st * jnp.exp(b_last) + kv_t
        ms = jnp.mean(o * o, axis=-1, keepdims=True)
        o = o * lax.rsqrt(ms + RMS_EPS) * gn_ref[...]
        r = r_ref[0, sl, :].astype(F32)
        o = o * (r / (1.0 + jnp.exp(-r)))
        o_ref[0, sl, :] = o.astype(o_ref.dtype)
    st_ref[...] = st


def _gla(proj, ga, w2p, gb, gn, *, heads, dk, dv, tc, col_q, col_k, col_v, col_r):
    bsz, t, _ = proj.shape
    kern = functools.partial(_gla_kernel, chunk=GLA_CHUNK, dk=dk, dv=dv)
    return pl.pallas_call(
        kern,
        grid=(bsz, heads, t // tc),
        in_specs=[
            pl.BlockSpec((1, tc, dk), lambda b, h, i: (b, i, col_q // dk + h)),
            pl.BlockSpec((1, tc, dk), lambda b, h, i: (b, i, col_k // dk + h)),
            pl.BlockSpec((1, tc, dv), lambda b, h, i: (b, i, col_v // dv + h)),
            pl.BlockSpec((1, tc, dv), lambda b, h, i: (b, i, col_r // dv + h)),
            pl.BlockSpec((1, tc, LANES), lambda b, h, i: (b, i, 0)),
            pl.BlockSpec((LANES, dk), lambda b, h, i: (0, h)),
            pl.BlockSpec((1, dk), lambda b, h, i: (0, h)),
            pl.BlockSpec((1, dv), lambda b, h, i: (0, 0)),
        ],
        out_specs=pl.BlockSpec((1, tc, dv), lambda b, h, i: (b, i, h)),
        out_shape=jax.ShapeDtypeStruct((bsz, t, heads * dv), BF16),
        scratch_shapes=[pltpu.VMEM((dv, dk), F32)],
        compiler_params=_cparams(("parallel", "parallel", "arbitrary")),
        name="gla",
    )(proj, proj, proj, proj, ga, w2p, gb, gn)


def _rope_table_kernel(pos_ref, freq_ref, cos_ref, sin_ref):
    ang = pos_ref[...] * freq_ref[...]
    cos_ref[...] = jnp.cos(ang)
    sin_ref[...] = jnp.sin(ang)


def _rope_tables(posf, freq, *, rows):
    n = posf.shape[0]
    spec = pl.BlockSpec((rows, LANES), lambda i: (i, 0))
    return pl.pallas_call(
        _rope_table_kernel,
        grid=(n // rows,),
        in_specs=[spec, pl.BlockSpec((1, LANES), lambda i: (0, 0))],
        out_specs=[spec, spec],
        out_shape=[jax.ShapeDtypeStruct((n, LANES), F32)] * 2,
        compiler_params=_cparams(("parallel",)),
        name="rope_tables",
    )(posf, freq)


def _rotate_half(x, first_half):
    half = SWA_HEAD_DIM // 2
    width = x.shape[-1]
    fwd = pltpu.roll(x, width - half, 1)
    bwd = pltpu.roll(x, half, 1)
    return jnp.where(first_half, -fwd, bwd)


def _swa_kernel(q_ref, kp_ref, kc_ref, vp_ref, vc_ref, cp_ref, sp_ref, cc_ref,
                sc_ref, sink_ref, o_ref, *, kv_heads):
    w = SWA_BLOCK
    d = SWA_HEAD_DIM
    blk = pl.program_id(1)
    lane = lax.broadcasted_iota(I32, (w, LANES), 1)
    first_half = (lane % d) < (d // 2)
    low_head = lane < d

    def rope(x, cos, sin, fh):
        return x * cos + _rotate_half(x, fh) * sin

    cos_c, sin_c = cc_ref[...], sc_ref[...]
    cos_p, sin_p = cp_ref[...], sp_ref[...]

    kw = kv_heads * d
    lane_k = lax.broadcasted_iota(I32, (w, kw), 1)
    fh_k = (lane_k % d) < (d // 2)
    reps = kw // LANES
    tile = lambda t: jnp.concatenate([t] * reps, axis=1) if reps > 1 else t
    k_prev = rope(kp_ref[0].astype(F32), tile(cos_p), tile(sin_p), fh_k)
    k_cur = rope(kc_ref[0].astype(F32), tile(cos_c), tile(sin_c), fh_k)
    k_all = jnp.concatenate([k_prev, k_cur], axis=0)
    v_all = jnp.concatenate([vp_ref[0], vc_ref[0]], axis=0)

    qi = lax.broadcasted_iota(I32, (w, 2 * w), 0) + w
    kj = lax.broadcasted_iota(I32, (w, 2 * w), 1)
    diff = qi - kj
    valid = (diff >= 0) & (diff < w) & ((kj >= w) | (blk > 0))

    scale = d ** -0.5
    pairs_per_kv = SWA_GROUP // 2
    for g in range(kv_heads):
        kg = k_all[:, g * d:(g + 1) * d]
        vg = v_all[:, g * d:(g + 1) * d]
        zk = jnp.zeros_like(kg)
        zv = jnp.zeros_like(vg)
        k_lo = jnp.concatenate([kg, zk], axis=1).astype(BF16)
        k_hi = jnp.concatenate([zk, kg], axis=1).astype(BF16)
        v_lo = jnp.concatenate([vg, zv], axis=1)
        v_hi = jnp.concatenate([zv, vg], axis=1)
        for p in range(pairs_per_kv):
            cb = g * pairs_per_kv + p
            q2 = q_ref[0, :, cb * LANES:(cb + 1) * LANES].astype(F32)
            q2 = (rope(q2, cos_c, sin_c, first_half) * scale).astype(BF16)
            sink2 = sink_ref[:, cb * LANES:(cb + 1) * LANES]
            acc = None
            for kk, vv, sel in ((k_lo, v_lo, low_head), (k_hi, v_hi, ~low_head)):
                s = lax.dot_general(q2, kk, (((1,), (1,)), ((), ())),
                                    preferred_element_type=F32)
                s = jnp.where(valid, s, NEG_BIG)
                sink = jnp.max(jnp.where(sel[:1], sink2, NEG_BIG), axis=-1, keepdims=True)
                m = jnp.maximum(jnp.max(s, axis=-1, keepdims=True), sink)
                p_un = jnp.exp(s - m)
                denom = jnp.sum(p_un, axis=-1, keepdims=True) + jnp.exp(sink - m)
                o = jnp.dot(p_un.astype(BF16), vv, preferred_element_type=F32)
                o = o / denom
                acc = o if acc is None else acc + o
            o_ref[0, :, cb * LANES:(cb + 1) * LANES] = acc.astype(o_ref.dtype)


def _swa(proj, cos, sin, sinks_lane, *, q_heads, kv_heads, col_q, col_k, col_v):
    bsz, t, _ = proj.shape
    w = SWA_BLOCK
    qw = q_heads * SWA_HEAD_DIM
    kw = kv_heads * SWA_HEAD_DIM
    prev = lambda i: jnp.maximum(i - 1, 0)
    kern = functools.partial(_swa_kernel, kv_heads=kv_heads)
    tab_p = pl.BlockSpec((w, LANES), lambda b, i: (b * (t // w) + prev(i), 0))
    tab_c = pl.BlockSpec((w, LANES), lambda b, i: (b * (t // w) + i, 0))
    return pl.pallas_call(
        kern,
        grid=(bsz, t // w),
        in_specs=[
            pl.BlockSpec((1, w, qw), lambda b, i: (b, i, col_q // qw)),
            pl.BlockSpec((1, w, kw), lambda b, i: (b, prev(i), col_k // kw)),
            pl.BlockSpec((1, w, kw), lambda b, i: (b, i, col_k // kw)),
            pl.BlockSpec((1, w, kw), lambda b, i: (b, prev(i), col_v // kw)),
            pl.BlockSpec((1, w, kw), lambda b, i: (b, i, col_v // kw)),
            tab_p, tab_p, tab_c, tab_c,
            pl.BlockSpec((1, qw), lambda b, i: (0, 0)),
        ],
        out_specs=pl.BlockSpec((1, w, qw), lambda b, i: (b, i, 0)),
        out_shape=jax.ShapeDtypeStruct((bsz, t, qw), BF16),
        compiler_params=_cparams(("parallel", "arbitrary")),
        name="swa",
    )(proj, proj, proj, proj, proj, cos, sin, cos, sin, sinks_lane)


def _router_kernel(h_ref, g_ref, wr_ref, br_ref, u_ref, rt_ref):
    h = h_ref[...]
    ms = jnp.mean(h * h, axis=-1, keepdims=True)
    u = h * lax.rsqrt(ms + RMS_EPS) * g_ref[...]
    u_ref[...] = u
    logits = jnp.dot(u, wr_ref[...], preferred_element_type=F32,
                     precision=lax.Precision.HIGHEST) + br_ref[...]
    lane = lax.broadcasted_iota(I32, logits.shape, 1)
    is_group = lane < N_GROUPS
    gl = jnp.where(is_group, logits, NEG_BIG)
    gmax = jnp.max(gl, axis=-1, keepdims=True)
    gidx = jnp.min(jnp.where(gl == gmax, lane, LANES), axis=-1, keepdims=True)
    gsum = jnp.sum(jnp.where(is_group, jnp.exp(gl - gmax), 0.0), axis=-1, keepdims=True)
    g_w = 1.0 / gsum
    lo = N_GROUPS + gidx * EXPERTS_PER_GROUP
    sel = (lane >= lo) & (lane < lo + EXPERTS_PER_GROUP)
    el = jnp.where(sel, logits, NEG_BIG)
    m1 = jnp.max(el, axis=-1, keepdims=True)
    i1 = jnp.min(jnp.where(el == m1, lane, LANES), axis=-1, keepdims=True)
    el2 = jnp.where(lane == i1, NEG_BIG, el)
    m2 = jnp.max(el2, axis=-1, keepdims=True)
    i2 = jnp.min(jnp.where(el2 == m2, lane, LANES), axis=-1, keepdims=True)
    t = jnp.exp(m2 - m1)
    w1 = g_w / (1.0 + t)
    w2 = g_w * t / (1.0 + t)
    e1 = (i1 - N_GROUPS).astype(F32)
    e2 = (i2 - N_GROUPS).astype(F32)
    rt_ref[...] = jnp.where(lane == 0, e1,
                  jnp.where(lane == 1, e2,
                  jnp.where(lane == 2, w1,
                  jnp.where(lane == 3, w2, 0.0))))


def _router(h2d, g, wr, br, *, rows):
    n, d = h2d.shape
    return pl.pallas_call(
        _router_kernel,
        grid=(n // rows,),
        in_specs=[pl.BlockSpec((rows, d), lambda i: (i, 0)),
                  pl.BlockSpec((1, d), lambda i: (0, 0)),
                  pl.BlockSpec((d, LANES), lambda i: (0, 0)),
                  pl.BlockSpec((1, LANES), lambda i: (0, 0))],
        out_specs=[pl.BlockSpec((rows, d), lambda i: (i, 0)),
                   pl.BlockSpec((rows, LANES), lambda i: (i, 0))],
        out_shape=[jax.ShapeDtypeStruct((n, d), F32),
                   jax.ShapeDtypeStruct((n, LANES), F32)],
        compiler_params=_cparams(("parallel",)),
        name="router",
    )(h2d, g, wr, br)


def _expert_kernel(te_ref, nu_ref, tv_ref, src_ref, nxt_ref, dst_ref, u_ref, wg_ref, wu_ref,
                   wd_ref, y_ref, xbuf, ybuf, gsem, ssem, *, tm):
    i = pl.program_id(0)
    n_used = nu_ref[0]

    def gather(idx_ref, slot):
        def issue(k, carry):
            t = idx_ref[0, 0, k]
            pltpu.make_async_copy(u_ref.at[pl.ds(t, 1), :],
                                  xbuf.at[slot, pl.ds(k, 1), :], gsem.at[slot]).start()
            return carry
        lax.fori_loop(0, tm, issue, 0, unroll=8)

    def gather_wait(slot):
        pltpu.make_async_copy(u_ref.at[pl.ds(0, tm), :], xbuf.at[slot], gsem.at[slot]).wait()

    def scatter(slot, n_valid):
        def issue(k, carry):
            r = dst_ref[0, 0, k]
            pltpu.make_async_copy(ybuf.at[slot, pl.ds(k, 1), :],
                                  y_ref.at[pl.ds(r, 1), :], ssem.at[slot]).start()
            return carry
        lax.fori_loop(0, n_valid, issue, 0)

    def scatter_wait(slot, n_valid):
        bit = tm
        while bit >= 1:
            @pl.when((n_valid & bit) != 0)
            def _(bit=bit):
                pltpu.make_async_copy(ybuf.at[slot, pl.ds(0, bit), :],
                                      y_ref.at[pl.ds(0, bit), :], ssem.at[slot]).wait()
            bit //= 2

    @pl.when(i < n_used)
    def _():
        slot = i % 2

        @pl.when(i == 0)
        def _():
            gather(src_ref, 0)

        @pl.when(i + 1 < n_used)
        def _():
            gather(nxt_ref, 1 - slot)

        gather_wait(slot)

        @pl.when(i >= 2)
        def _():
            scatter_wait(slot, tv_ref[i - 2])

        x = xbuf[slot].astype(BF16)
        a = jnp.dot(x, wg_ref[0], preferred_element_type=F32)
        b = jnp.dot(x, wu_ref[0], preferred_element_type=F32)
        hid = (a / (1.0 + jnp.exp(-a)) * b).astype(BF16)
        ybuf[slot] = jnp.dot(hid, wd_ref[0], preferred_element_type=F32)
        scatter(slot, tv_ref[i])

        @pl.when(i == n_used - 1)
        def _():
            scatter_wait(slot, tv_ref[i])

            @pl.when(i >= 1)
            def _():
                scatter_wait(1 - slot, tv_ref[i - 1])


def _experts(tile_expert, n_used, tile_valid, src_tok, out_row, u2, wg, wu, wd, n_out_rows,
             *, tm):
    n_tiles = src_tok.shape[0]
    _, d = u2.shape
    _, _, f = wg.shape
    kern = functools.partial(_expert_kernel, tm=tm)
    idx_spec = lambda off: pl.BlockSpec(
        (1, 1, tm), lambda i, te, nu, tv: (jnp.minimum(i + off, n_tiles - 1), 0, 0),
        memory_space=pltpu.SMEM)
    w_map = lambda i, te, nu, tv: (te[i], 0, 0)
    grid_spec = pltpu.PrefetchScalarGridSpec(
        num_scalar_prefetch=3,
        grid=(n_tiles,),
        in_specs=[idx_spec(0), idx_spec(1), idx_spec(0),
                  pl.BlockSpec(memory_space=pl.ANY),
                  pl.BlockSpec((1, d, f), w_map),
                  pl.BlockSpec((1, d, f), w_map),
                  pl.BlockSpec((1, f, d), w_map)],
        out_specs=pl.BlockSpec(memory_space=pl.ANY),
        scratch_shapes=[pltpu.VMEM((2, tm, d), F32), pltpu.VMEM((2, tm, d), F32),
                        pltpu.SemaphoreType.DMA((2,)), pltpu.SemaphoreType.DMA((2,))],
    )
    return pl.pallas_call(
        kern,
        grid_spec=grid_spec,
        out_shape=jax.ShapeDtypeStruct((n_out_rows, d), F32),
        compiler_params=_cparams(("arbitrary",)),
        name="experts",
    )(tile_expert, n_used, tile_valid, src_tok, src_tok, out_row, u2, wg, wu, wd)


def _combine_kernel(h_ref, y0_ref, y1_ref, rt_ref, g_ref, o_ref, *, final_norm):
    rt = rt_ref[...]
    w0 = rt[:, 2:3]
    w1 = rt[:, 3:4]
    h = h_ref[...] + w0 * y0_ref[...].astype(F32) + w1 * y1_ref[...].astype(F32)
    if final_norm:
        ms = jnp.mean(h * h, axis=-1, keepdims=True)
        h = h * lax.rsqrt(ms + RMS_EPS) * g_ref[...]
    o_ref[...] = h


def _combine(h2d, y, route, g, *, rows, final_norm):
    n, d = h2d.shape
    nb = n // rows
    return pl.pallas_call(
        functools.partial(_combine_kernel, final_norm=final_norm),
        grid=(nb,),
        in_specs=[pl.BlockSpec((rows, d), lambda i: (i, 0)),
                  pl.BlockSpec((rows, d), lambda i: (i, 0)),
                  pl.BlockSpec((rows, d), lambda i: (nb + i, 0)),
                  pl.BlockSpec((rows, LANES), lambda i: (i, 0)),
                  pl.BlockSpec((1, d), lambda i: (0, 0))],
        out_specs=pl.BlockSpec((rows, d), lambda i: (i, 0)),
        out_shape=jax.ShapeDtypeStruct((n, d), F32),
        compiler_params=_cparams(("parallel",)),
        name="combine",
    )(h2d, y, y, route, g)


def _dispatch_plan(eid, n_experts, tm, p_rows):
    n = eid.shape[0]
    n_flat = TOP_K * n
    n_tiles = p_rows // tm
    e_flat = eid.T.reshape(-1)
    onehot = (e_flat[:, None] == jnp.arange(n_experts, dtype=I32)[None, :]).astype(I32)
    csum = jnp.cumsum(onehot, axis=0)
    counts = csum[-1]
    pcounts = ((counts + tm - 1) // tm) * tm
    pend = jnp.cumsum(pcounts)
    pstart = pend - pcounts
    dest = jnp.sum(onehot * (csum - 1 + pstart[None, :]), axis=1)
    n_used = (pend[-1] // tm).astype(I32)
    inv = jnp.full((p_rows,), n_flat, I32).at[dest].set(
        jnp.arange(n_flat, dtype=I32), unique_indices=True)
    is_pad = inv >= n_flat
    src_tok = jnp.where(is_pad, 0, inv % n)
    out_row = jnp.where(is_pad, 0, inv)
    tile_start = jnp.arange(n_tiles, dtype=I32) * tm
    used_start = jnp.minimum(tile_start, pend[-1] - tm)
    tile_onehot = ((used_start[:, None] >= pstart[None, :])
                   & (used_start[:, None] < pend[None, :])).astype(I32)
    tile_expert = jnp.sum(tile_onehot * jnp.arange(n_experts, dtype=I32)[None, :], axis=1)
    valid_end = jnp.sum(tile_onehot * (pstart + counts)[None, :], axis=1)
    tile_valid = jnp.where(tile_start < pend[-1], jnp.clip(valid_end - tile_start, 0, tm), 0)
    return (src_tok.reshape(n_tiles, 1, tm), out_row.reshape(n_tiles, 1, tm),
            tile_expert, n_used.reshape(1), tile_valid.astype(I32))


def _tiles(n_tokens, d_model):
    return dict(
        norm_rows=256,
        mm_tm=1024, mm_tn=512,
        gla_tc=512,
        moe_tm=256,
    )


def kernel(x, positions, norm_mix_g, w_in, gla_gate_w2, gla_gate_b, gla_out_norm_g,
           swa_sinks, w_out, norm_ffn_g, router_group_w, router_group_b,
           router_expert_w, router_expert_b, expert_w_gate, expert_w_up,
           expert_w_down, norm_final_g):
    bsz, t, d = x.shape
    n = bsz * t
    depth = w_in.shape[0]
    cfg = _tiles(n, d)

    rank = gla_gate_w2.shape[1]
    key_w = gla_gate_w2.shape[2]
    val_w = GLA_HEADS * gla_out_norm_g.shape[1]
    dk, dv = key_w // GLA_HEADS, val_w // GLA_HEADS
    q_heads = swa_sinks.shape[1]
    kv_heads = q_heads // SWA_GROUP
    sq_w, skv_w = q_heads * SWA_HEAD_DIM, kv_heads * SWA_HEAD_DIM
    n_experts = expert_w_gate.shape[1]
    tm = cfg["moe_tm"]
    p_rows = TOP_K * n + n_experts * tm

    c_gq, c_gk, c_gv, c_gr = 0, key_w, 2 * key_w, 2 * key_w + val_w
    c_ga = 2 * key_w + 2 * val_w
    c_sq = c_ga + rank
    m_sq = c_ga
    m_sk, m_sv = m_sq + sq_w, m_sq + sq_w + skv_w

    half = SWA_HEAD_DIM // 2
    inv_freq = ROPE_THETA ** (-jnp.arange(half, dtype=F32) / half)
    freq_lane = jnp.tile(inv_freq, LANES // half).reshape(1, LANES)
    posf = jnp.broadcast_to(positions.astype(F32).reshape(n, 1), (n, LANES))
    cos_t, sin_t = _rope_tables(posf, freq_lane, rows=1024)

    h2d = x.reshape(n, d)
    for l in range(depth):
        w_l = w_in[l]
        w_main = jnp.concatenate([w_l[:, :c_ga], w_l[:, c_sq:]], axis=1).astype(BF16)
        w_ga = jnp.pad(w_l[:, c_ga:c_sq], ((0, 0), (0, LANES - rank))).astype(BF16)
        w2p = jnp.pad(gla_gate_w2[l], ((0, LANES - rank), (0, 0))).astype(BF16)

        u, ga = _norm_gate(h2d, norm_mix_g[l].reshape(1, d), w_ga, rows=cfg["norm_rows"])
        proj = _matmul(u, w_main, tm=cfg["mm_tm"], tn=cfg["mm_tn"], out_dtype=BF16,
                       name="in_proj")
        proj3 = proj.reshape(bsz, t, -1)

        o_gla = _gla(proj3, ga.reshape(bsz, t, LANES), w2p,
                     gla_gate_b[l].reshape(1, key_w), gla_out_norm_g[l].reshape(1, dv),
                     heads=GLA_HEADS, dk=dk, dv=dv, tc=cfg["gla_tc"],
                     col_q=c_gq, col_k=c_gk, col_v=c_gv, col_r=c_gr)
        sinks_lane = jnp.repeat(swa_sinks[l], SWA_HEAD_DIM).reshape(1, sq_w)
        o_swa = _swa(proj3, cos_t, sin_t, sinks_lane, q_heads=q_heads, kv_heads=kv_heads,
                     col_q=m_sq, col_k=m_sk, col_v=m_sv)

        h2d = _out_proj(o_gla.reshape(n, val_w), o_swa.reshape(n, sq_w),
                        w_out[l].astype(BF16), h2d, tm=cfg["mm_tm"], tn=cfg["mm_tn"])

        wr = jnp.concatenate(
            [router_group_w[l],
             jnp.transpose(router_expert_w[l], (1, 0, 2)).reshape(d, n_experts)], axis=1)
        n_router = wr.shape[1]
        wr = jnp.pad(wr, ((0, 0), (0, LANES - n_router)))
        br = jnp.pad(jnp.concatenate([router_group_b[l], router_expert_b[l].reshape(-1)]),
                     (0, LANES - n_router)).reshape(1, LANES)
        u2, route = _router(h2d, norm_ffn_g[l].reshape(1, d), wr, br, rows=cfg["norm_rows"])

        eid = route[:, :TOP_K].astype(I32)
        src_tok, out_row, tile_expert, n_used, tile_valid = _dispatch_plan(
            eid, n_experts, tm, p_rows)
        y_tok = _experts(tile_expert, n_used, tile_valid, src_tok, out_row, u2,
                         expert_w_gate[l].astype(BF16), expert_w_up[l].astype(BF16),
                         expert_w_down[l].astype(BF16), TOP_K * n, tm=tm)

        h2d = _combine(h2d, y_tok, route, norm_final_g.reshape(1, d),
                       rows=cfg["norm_rows"], final_norm=(l == depth - 1))
    return h2d.reshape(bsz, t, d)
```

```python
import functools

import jax
import jax.numpy as jnp
from jax import lax
from jax.experimental import pallas as pl
from jax.experimental.pallas import tpu as pltpu

F32 = jnp.float32
BF16 = jnp.bfloat16
I32 = jnp.int32

RMS_EPS = 1e-6

GLA_HEADS = 8
GLA_GATE_TAU = 16.0
GLA_CHUNK = 64
SWA_HEAD_DIM = 64
SWA_GROUP = 8
SWA_BLOCK = 128
ROPE_THETA = 10000.0
N_GROUPS = 8
EXPERTS_PER_GROUP = 8
TOP_K = 2

LANES = 128
VMEM_LIMIT_BYTES = 56 * 1024 * 1024

NEG_BIG = -1e30


def _cparams(semantics):
    return pltpu.CompilerParams(dimension_semantics=semantics,
                                vmem_limit_bytes=VMEM_LIMIT_BYTES)


def _norm_gate_kernel(x_ref, g_ref, wga_ref, u_ref, ga_ref):
    x = x_ref[...]
    ms = jnp.mean(x * x, axis=-1, keepdims=True)
    u = (x * lax.rsqrt(ms + RMS_EPS) * g_ref[...]).astype(BF16)
    u_ref[...] = u
    ga_ref[...] = jnp.dot(u, wga_ref[...], preferred_element_type=F32)


def _norm_gate(x2d, g, wga, *, rows):
    n, d = x2d.shape
    return pl.pallas_call(
        _norm_gate_kernel,
        grid=(n // rows,),
        in_specs=[pl.BlockSpec((rows, d), lambda i: (i, 0)),
                  pl.BlockSpec((1, d), lambda i: (0, 0)),
                  pl.BlockSpec((d, LANES), lambda i: (0, 0))],
        out_specs=[pl.BlockSpec((rows, d), lambda i: (i, 0)),
                   pl.BlockSpec((rows, LANES), lambda i: (i, 0))],
        out_shape=[jax.ShapeDtypeStruct((n, d), BF16),
                   jax.ShapeDtypeStruct((n, LANES), F32)],
        compiler_params=_cparams(("parallel",)),
        name="norm_gate",
    )(x2d, g, wga)


def _matmul_kernel(a_ref, w_ref, o_ref):
    o_ref[...] = jnp.dot(a_ref[...], w_ref[...],
                         preferred_element_type=F32).astype(o_ref.dtype)


def _matmul(a, w, *, tm, tn, out_dtype, name):
    m, k = a.shape
    _, n = w.shape
    return pl.pallas_call(
        _matmul_kernel,
        grid=(m // tm, n // tn),
        in_specs=[pl.BlockSpec((tm, k), lambda i, j: (i, 0)),
                  pl.BlockSpec((k, tn), lambda i, j: (0, j))],
        out_specs=pl.BlockSpec((tm, tn), lambda i, j: (i, j)),
        out_shape=jax.ShapeDtypeStruct((m, n), out_dtype),
        compiler_params=_cparams(("parallel", "arbitrary")),
        name=name,
    )(a, w)


def _out_proj_kernel(a_ref, b_ref, wa_ref, wb_ref, x_ref, o_ref):
    acc = jnp.dot(a_ref[...], wa_ref[...], preferred_element_type=F32)
    acc = acc + jnp.dot(b_ref[...], wb_ref[...], preferred_element_type=F32)
    o_ref[...] = x_ref[...] + acc


def _out_proj(a, b, w, x2d, *, tm, tn):
    m, ka = a.shape
    _, kb = b.shape
    _, n = w.shape
    return pl.pallas_call(
        _out_proj_kernel,
        grid=(m // tm, n // tn),
        in_specs=[pl.BlockSpec((tm, ka), lambda i, j: (i, 0)),
                  pl.BlockSpec((tm, kb), lambda i, j: (i, 0)),
                  pl.BlockSpec((ka, tn), lambda i, j: (0, j)),
                  pl.BlockSpec((kb, tn), lambda i, j: (ka // kb, j)),
                  pl.BlockSpec((tm, tn), lambda i, j: (i, j))],
        out_specs=pl.BlockSpec((tm, tn), lambda i, j: (i, j)),
        out_shape=jax.ShapeDtypeStruct((m, n), F32),
        compiler_params=_cparams(("parallel", "arbitrary")),
        name="out_proj",
    )(a, b, w, w, x2d)


def _gla_kernel(q_ref, k_ref, v_ref, r_ref, ga_ref, w2_ref, gb_ref, gn_ref, tri_ref,
                o_ref, st_ref, *, chunk, dk, dv):
    @pl.when(pl.program_id(2) == 0)
    def _():
        st_ref[...] = jnp.zeros_like(st_ref)

    tc = q_ref.shape[1]
    n_chunks = tc // chunk
    rows = lambda a, c: a[c * chunk:(c + 1) * chunk, :]
    last = lambda a, c: a[(c + 1) * chunk - 1:(c + 1) * chunk, :]
    contract_last = (((1,), (1,)), ((), ()))
    contract_first = (((0,), (0,)), ((), ()))

    tri = tri_ref[...]
    tri_b = tri.astype(BF16)
    q = q_ref[0].astype(F32) * (dk ** -0.5)
    k = k_ref[0].astype(F32)
    v = v_ref[0]
    z = jnp.dot(ga_ref[0].astype(BF16), w2_ref[...], preferred_element_type=F32) + gb_ref[...]
    g = -(jnp.maximum(-z, 0.0) + jnp.log1p(jnp.exp(-jnp.abs(z)))) / GLA_GATE_TAU
    g_hi = g.astype(BF16)
    g_lo = (g - g_hi.astype(F32)).astype(BF16)
    b = (jnp.dot(tri_b, g_hi, preferred_element_type=F32)
         + jnp.dot(tri_b, g_lo, preferred_element_type=F32))
    q_in = (q * jnp.exp(b)).astype(BF16)
    k_in = (k * jnp.exp(-b)).astype(BF16)
    a = lax.dot_general(q_in, k_in, contract_last, preferred_element_type=F32)
    a = jnp.where(tri > 0.0, a, 0.0).astype(BF16)
    o = jnp.dot(a, v, preferred_element_type=F32)

    kv_t = []
    for c in range(n_chunks):
        k_dec = (rows(k, c) * jnp.exp(last(b, c) - rows(b, c))).astype(BF16)
        kv_t.append(lax.dot_general(rows(v, c), k_dec, contract_first,
                                    preferred_element_type=F32))
    st = st_ref[...]
    st_in = []
    for c in range(n_chunks):
        st_in.append(st.astype(BF16))
        st = st * jnp.exp(last(b, c)) + kv_t[c]
    st_ref[...] = st
    o_inter = [lax.dot_general(rows(q_in, c), st_in[c], contract_last,
                               preferred_element_type=F32) for c in range(n_chunks)]
    o = o + jnp.concatenate(o_inter, axis=0)

    ms = jnp.mean(o * o, axis=-1, keepdims=True)
    o = o * lax.rsqrt(ms + RMS_EPS) * gn_ref[...]
    r = r_ref[0].astype(F32)
    o = o * (r / (1.0 + jnp.exp(-r)))
    o_ref[0] = o.astype(o_ref.dtype)


def _gla(proj, ga, w2p, gb, gn, *, heads, dk, dv, tc, col_q, col_k, col_v, col_r):
    bsz, t, _ = proj.shape
    kern = functools.partial(_gla_kernel, chunk=GLA_CHUNK, dk=dk, dv=dv)
    idx = jnp.arange(tc, dtype=I32)
    tri = ((idx[:, None] // GLA_CHUNK == idx[None, :] // GLA_CHUNK)
           & (idx[None, :] <= idx[:, None])).astype(F32)
    return pl.pallas_call(
        kern,
        grid=(bsz, heads, t // tc),
        in_specs=[
            pl.BlockSpec((1, tc, dk), lambda b, h, i: (b, i, col_q // dk + h)),
            pl.BlockSpec((1, tc, dk), lambda b, h, i: (b, i, col_k // dk + h)),
            pl.BlockSpec((1, tc, dv), lambda b, h, i: (b, i, col_v // dv + h)),
            pl.BlockSpec((1, tc, dv), lambda b, h, i: (b, i, col_r // dv + h)),
            pl.BlockSpec((1, tc, LANES), lambda b, h, i: (b, i, 0)),
            pl.BlockSpec((LANES, dk), lambda b, h, i: (0, h)),
            pl.BlockSpec((1, dk), lambda b, h, i: (0, h)),
            pl.BlockSpec((1, dv), lambda b, h, i: (0, 0)),
            pl.BlockSpec((tc, tc), lambda b, h, i: (0, 0)),
        ],
        out_specs=pl.BlockSpec((1, tc, dv), lambda b, h, i: (b, i, h)),
        out_shape=jax.ShapeDtypeStruct((bsz, t, heads * dv), BF16),
        scratch_shapes=[pltpu.VMEM((dv, dk), F32)],
        compiler_params=_cparams(("parallel", "parallel", "arbitrary")),
        name="gla",
    )(proj, proj, proj, proj, ga, w2p, gb, gn, tri)


def _rope_table_kernel(pos_ref, freq_ref, cos_ref, sin_ref):
    ang = pos_ref[...] * freq_ref[...]
    cos_ref[...] = jnp.cos(ang)
    sin_ref[...] = jnp.sin(ang)


def _rope_tables(posf, freq, *, rows):
    n = posf.shape[0]
    spec = pl.BlockSpec((rows, LANES), lambda i: (i, 0))
    return pl.pallas_call(
        _rope_table_kernel,
        grid=(n // rows,),
        in_specs=[spec, pl.BlockSpec((1, LANES), lambda i: (0, 0))],
        out_specs=[spec, spec],
        out_shape=[jax.ShapeDtypeStruct((n, LANES), F32)] * 2,
        compiler_params=_cparams(("parallel",)),
        name="rope_tables",
    )(posf, freq)


def _rotate_half(x, first_half):
    half = SWA_HEAD_DIM // 2
    width = x.shape[-1]
    fwd = pltpu.roll(x, width - half, 1)
    bwd = pltpu.roll(x, half, 1)
    return jnp.where(first_half, -fwd, bwd)


def _swa_kernel(q_ref, kp_ref, kc_ref, vp_ref, vc_ref, cp_ref, sp_ref, cc_ref,
                sc_ref, sink_ref, o_ref, *, kv_heads):
    w = SWA_BLOCK
    d = SWA_HEAD_DIM
    blk = pl.program_id(1)
    lane = lax.broadcasted_iota(I32, (w, LANES), 1)
    first_half = (lane % d) < (d // 2)
    low_head = lane < d

    def rope(x, cos, sin, fh):
        return x * cos + _rotate_half(x, fh) * sin

    cos_c, sin_c = cc_ref[...], sc_ref[...]
    cos_p, sin_p = cp_ref[...], sp_ref[...]

    kw = kv_heads * d
    lane_k = lax.broadcasted_iota(I32, (w, kw), 1)
    fh_k = (lane_k % d) < (d // 2)
    reps = kw // LANES
    tile = lambda t: jnp.concatenate([t] * reps, axis=1) if reps > 1 else t
    k_prev = rope(kp_ref[0].astype(F32), tile(cos_p), tile(sin_p), fh_k)
    k_cur = rope(kc_ref[0].astype(F32), tile(cos_c), tile(sin_c), fh_k)
    k_all = jnp.concatenate([k_prev, k_cur], axis=0)
    v_all = jnp.concatenate([vp_ref[0], vc_ref[0]], axis=0)

    pairs_per_kv = SWA_GROUP // 2
    m_rows = pairs_per_kv * w
    qi = lax.broadcasted_iota(I32, (m_rows, 2 * w), 0) % w + w
    kj = lax.broadcasted_iota(I32, (m_rows, 2 * w), 1)
    diff = qi - kj
    valid = (diff >= 0) & (diff < w) & ((kj >= w) | (blk > 0))

    scale = d ** -0.5
    for g in range(kv_heads):
        kg = k_all[:, g * d:(g + 1) * d]
        vg = v_all[:, g * d:(g + 1) * d]
        zk = jnp.zeros_like(kg)
        zv = jnp.zeros_like(vg)
        k_lo = jnp.concatenate([kg, zk], axis=1).astype(BF16)
        k_hi = jnp.concatenate([zk, kg], axis=1).astype(BF16)
        v_lo = jnp.concatenate([vg, zv], axis=1)
        v_hi = jnp.concatenate([zv, vg], axis=1)
        q_parts, sink_lo, sink_hi = [], [], []
        for p in range(pairs_per_kv):
            cb = g * pairs_per_kv + p
            q2 = q_ref[0, :, cb * LANES:(cb + 1) * LANES].astype(F32)
            q_parts.append((rope(q2, cos_c, sin_c, first_half) * scale).astype(BF16))
            sink2 = sink_ref[:, cb * LANES:(cb + 1) * LANES]
            for sel, dst in ((low_head, sink_lo), (~low_head, sink_hi)):
                s1 = jnp.max(jnp.where(sel[:1], sink2, NEG_BIG), axis=-1, keepdims=True)
                dst.append(jnp.broadcast_to(s1, (w, 1)))
        q_st = jnp.concatenate(q_parts, axis=0)
        acc = None
        for kk, vv, sinks in ((k_lo, v_lo, sink_lo), (k_hi, v_hi, sink_hi)):
            sink = jnp.concatenate(sinks, axis=0)
            s = lax.dot_general(q_st, kk, (((1,), (1,)), ((), ())),
                                preferred_element_type=F32)
            s = jnp.where(valid, s, NEG_BIG)
            m = jnp.maximum(jnp.max(s, axis=-1, keepdims=True), sink)
            p_un = jnp.exp(s - m)
            denom = jnp.sum(p_un, axis=-1, keepdims=True) + jnp.exp(sink - m)
            o = jnp.dot(p_un.astype(BF16), vv, preferred_element_type=F32)
            o = o / denom
            acc = o if acc is None else acc + o
        for p in range(pairs_per_kv):
            cb = g * pairs_per_kv + p
            o_ref[0, :, cb * LANES:(cb + 1) * LANES] = acc[p * w:(p + 1) * w].astype(o_ref.dtype)


def _swa(proj, cos, sin, sinks_lane, *, q_heads, kv_heads, col_q, col_k, col_v):
    bsz, t, _ = proj.shape
    w = SWA_BLOCK
    qw = q_heads * SWA_HEAD_DIM
    kw = kv_heads * SWA_HEAD_DIM
    prev = lambda i: jnp.maximum(i - 1, 0)
    kern = functools.partial(_swa_kernel, kv_heads=kv_heads)
    tab_p = pl.BlockSpec((w, LANES), lambda b, i: (b * (t // w) + prev(i), 0))
    tab_c = pl.BlockSpec((w, LANES), lambda b, i: (b * (t // w) + i, 0))
    return pl.pallas_call(
        kern,
        grid=(bsz, t // w),
        in_specs=[
            pl.BlockSpec((1, w, qw), lambda b, i: (b, i, col_q // qw)),
            pl.BlockSpec((1, w, kw), lambda b, i: (b, prev(i), col_k // kw)),
            pl.BlockSpec((1, w, kw), lambda b, i: (b, i, col_k // kw)),
            pl.BlockSpec((1, w, kw), lambda b, i: (b, prev(i), col_v // kw)),
            pl.BlockSpec((1, w, kw), lambda b, i: (b, i, col_v // kw)),
            tab_p, tab_p, tab_c, tab_c,
            pl.BlockSpec((1, qw), lambda b, i: (0, 0)),
        ],
        out_specs=pl.BlockSpec((1, w, qw), lambda b, i: (b, i, 0)),
        out_shape=jax.ShapeDtypeStruct((bsz, t, qw), BF16),
        compiler_params=_cparams(("parallel", "arbitrary")),
        name="swa",
    )(proj, proj, proj, proj, proj, cos, sin, cos, sin, sinks_lane)


def _router_kernel(h_ref, g_ref, wr_ref, br_ref, u_ref, rt_ref):
    h = h_ref[...]
    ms = jnp.mean(h * h, axis=-1, keepdims=True)
    u = h * lax.rsqrt(ms + RMS_EPS) * g_ref[...]
    u_ref[...] = u
    logits = jnp.dot(u, wr_ref[...], preferred_element_type=F32,
                     precision=lax.Precision.HIGHEST) + br_ref[...]
    lane = lax.broadcasted_iota(I32, logits.shape, 1)
    is_group = lane < N_GROUPS
    gl = jnp.where(is_group, logits, NEG_BIG)
    gmax = jnp.max(gl, axis=-1, keepdims=True)
    gidx = jnp.min(jnp.where(gl == gmax, lane, LANES), axis=-1, keepdims=True)
    gsum = jnp.sum(jnp.where(is_group, jnp.exp(gl - gmax), 0.0), axis=-1, keepdims=True)
    g_w = 1.0 / gsum
    lo = N_GROUPS + gidx * EXPERTS_PER_GROUP
    sel = (lane >= lo) & (lane < lo + EXPERTS_PER_GROUP)
    el = jnp.where(sel, logits, NEG_BIG)
    m1 = jnp.max(el, axis=-1, keepdims=True)
    i1 = jnp.min(jnp.where(el == m1, lane, LANES), axis=-1, keepdims=True)
    el2 = jnp.where(lane == i1, NEG_BIG, el)
    m2 = jnp.max(el2, axis=-1, keepdims=True)
    i2 = jnp.min(jnp.where(el2 == m2, lane, LANES), axis=-1, keepdims=True)
    t = jnp.exp(m2 - m1)
    w1 = g_w / (1.0 + t)
    w2 = g_w * t / (1.0 + t)
    e1 = (i1 - N_GROUPS).astype(F32)
    e2 = (i2 - N_GROUPS).astype(F32)
    rt_ref[...] = jnp.where(lane == 0, e1,
                  jnp.where(lane == 1, e2,
                  jnp.where(lane == 2, w1,
                  jnp.where(lane == 3, w2, 0.0))))


def _router(h2d, g, wr, br, *, rows):
    n, d = h2d.shape
    return pl.pallas_call(
        _router_kernel,
        grid=(n // rows,),
        in_specs=[pl.BlockSpec((rows, d), lambda i: (i, 0)),
                  pl.BlockSpec((1, d), lambda i: (0, 0)),
                  pl.BlockSpec((d, LANES), lambda i: (0, 0)),
                  pl.BlockSpec((1, LANES), lambda i: (0, 0))],
        out_specs=[pl.BlockSpec((rows, d), lambda i: (i, 0)),
                   pl.BlockSpec((rows, LANES), lambda i: (i, 0))],
        out_shape=[jax.ShapeDtypeStruct((n, d), F32),
                   jax.ShapeDtypeStruct((n, LANES), F32)],
        compiler_params=_cparams(("parallel",)),
        name="router",
    )(h2d, g, wr, br)


def _expert_kernel(te_ref, nu_ref, tv_ref, src_ref, nxt_ref, dst_ref, u_ref, wg_ref, wu_ref,
                   wd_ref, y_ref, xbuf, ybuf, gsem, ssem, *, tm):
    i = pl.program_id(0)
    part = pl.program_id(1)
    n_used = nu_ref[0]

    def gather(idx_ref, slot):
        def issue(k, carry):
            t = idx_ref[0, 0, k]
            pltpu.make_async_copy(u_ref.at[pl.ds(t, 1), :],
                                  xbuf.at[slot, pl.ds(k, 1), :], gsem.at[slot]).start()
            return carry
        lax.fori_loop(0, tm, issue, 0, unroll=8)

    def gather_wait(slot):
        pltpu.make_async_copy(u_ref.at[pl.ds(0, tm), :], xbuf.at[slot], gsem.at[slot]).wait()

    def scatter(slot, n_valid):
        def issue(k, carry):
            r = dst_ref[0, 0, k]
            pltpu.make_async_copy(ybuf.at[slot, pl.ds(k, 1), :],
                                  y_ref.at[pl.ds(r, 1), :], ssem.at[slot]).start()
            return carry
        lax.fori_loop(0, n_valid, issue, 0)

    def scatter_wait(slot, n_valid):
        bit = tm
        while bit >= 1:
            @pl.when((n_valid & bit) != 0)
            def _(bit=bit):
                pltpu.make_async_copy(ybuf.at[slot, pl.ds(0, bit), :],
                                      y_ref.at[pl.ds(0, bit), :], ssem.at[slot]).wait()
            bit //= 2

    @pl.when(i < n_used)
    def _():
        slot = i % 2

        @pl.when(part == 0)
        def _():
            @pl.when(i == 0)
            def _():
                gather(src_ref, 0)

            @pl.when(i + 1 < n_used)
            def _():
                gather(nxt_ref, 1 - slot)

            gather_wait(slot)

            @pl.when(i >= 2)
            def _():
                scatter_wait(slot, tv_ref[i - 2])

        x = xbuf[slot].astype(BF16)
        a = jnp.dot(x, wg_ref[0].astype(BF16), preferred_element_type=F32)
        b = jnp.dot(x, wu_ref[0].astype(BF16), preferred_element_type=F32)
        hid = (a / (1.0 + jnp.exp(-a)) * b).astype(BF16)
        y = jnp.dot(hid, wd_ref[0].astype(BF16), preferred_element_type=F32)

        @pl.when(part == 0)
        def _():
            ybuf[slot] = y

        @pl.when(part == 1)
        def _():
            ybuf[slot] = ybuf[slot] + y
            scatter(slot, tv_ref[i])

            @pl.when(i == n_used - 1)
            def _():
                scatter_wait(slot, tv_ref[i])

                @pl.when(i >= 1)
                def _():
                    scatter_wait(1 - slot, tv_ref[i - 1])


def _experts(tile_expert, n_used, tile_valid, src_tok, out_row, u2, wg, wu, wd, n_out_rows,
             *, tm):
    n_tiles = src_tok.shape[0]
    _, d = u2.shape
    _, _, f = wg.shape
    kern = functools.partial(_expert_kernel, tm=tm)
    idx_spec = lambda off: pl.BlockSpec(
        (1, 1, tm), lambda i, p, te, nu, tv: (jnp.minimum(i + off, n_tiles - 1), 0, 0),
        memory_space=pltpu.SMEM)
    def half(i, p, nu):
        last = nu[0] - 1
        return jnp.where(i <= last, (i + p) % 2, (last + 1) % 2)
    fh = f // 2
    grid_spec = pltpu.PrefetchScalarGridSpec(
        num_scalar_prefetch=3,
        grid=(n_tiles, 2),
        in_specs=[idx_spec(0), idx_spec(1), idx_spec(0),
                  pl.BlockSpec(memory_space=pl.ANY),
                  pl.BlockSpec((1, d, fh), lambda i, p, te, nu, tv: (te[i], 0, half(i, p, nu))),
                  pl.BlockSpec((1, d, fh), lambda i, p, te, nu, tv: (te[i], 0, half(i, p, nu))),
                  pl.BlockSpec((1, fh, d), lambda i, p, te, nu, tv: (te[i], half(i, p, nu), 0))],
        out_specs=pl.BlockSpec(memory_space=pl.ANY),
        scratch_shapes=[pltpu.VMEM((2, tm, d), F32), pltpu.VMEM((2, tm, d), F32),
                        pltpu.SemaphoreType.DMA((2,)), pltpu.SemaphoreType.DMA((2,))],
    )
    return pl.pallas_call(
        kern,
        grid_spec=grid_spec,
        out_shape=jax.ShapeDtypeStruct((n_out_rows, d), F32),
        compiler_params=_cparams(("arbitrary", "arbitrary")),
        name="experts",
    )(tile_expert, n_used, tile_valid, src_tok, src_tok, out_row, u2, wg, wu, wd)


def _combine_kernel(h_ref, y0_ref, y1_ref, rt_ref, g_ref, o_ref, *, final_norm):
    rt = rt_ref[...]
    w0 = rt[:, 2:3]
    w1 = rt[:, 3:4]
    h = h_ref[...] + w0 * y0_ref[...].astype(F32) + w1 * y1_ref[...].astype(F32)
    if final_norm:
        ms = jnp.mean(h * h, axis=-1, keepdims=True)
        h = h * lax.rsqrt(ms + RMS_EPS) * g_ref[...]
    o_ref[...] = h


def _combine(h2d, y, route, g, *, rows, final_norm):
    n, d = h2d.shape
    nb = n // rows
    return pl.pallas_call(
        functools.partial(_combine_kernel, final_norm=final_norm),
        grid=(nb,),
        in_specs=[pl.BlockSpec((rows, d), lambda i: (i, 0)),
                  pl.BlockSpec((rows, d), lambda i: (i, 0)),
                  pl.BlockSpec((rows, d), lambda i: (nb + i, 0)),
                  pl.BlockSpec((rows, LANES), lambda i: (i, 0)),
                  pl.BlockSpec((1, d), lambda i: (0, 0))],
        out_specs=pl.BlockSpec((rows, d), lambda i: (i, 0)),
        out_shape=jax.ShapeDtypeStruct((n, d), F32),
        compiler_params=_cparams(("parallel",)),
        name="combine",
    )(h2d, y, y, route, g)


def _dispatch_plan(eid, n_experts, tm, p_rows):
    n = eid.shape[0]
    n_flat = TOP_K * n
    n_tiles = p_rows // tm
    e_flat = eid.T.reshape(-1)
    onehot = (e_flat[:, None] == jnp.arange(n_experts, dtype=I32)[None, :]).astype(I32)
    csum = jnp.cumsum(onehot, axis=0)
    counts = csum[-1]
    pcounts = ((counts + tm - 1) // tm) * tm
    pend = jnp.cumsum(pcounts)
    pstart = pend - pcounts
    dest = jnp.sum(onehot * (csum - 1 + pstart[None, :]), axis=1)
    n_used = (pend[-1] // tm).astype(I32)
    inv = jnp.full((p_rows,), n_flat, I32).at[dest].set(
        jnp.arange(n_flat, dtype=I32), unique_indices=True)
    is_pad = inv >= n_flat
    src_tok = jnp.where(is_pad, 0, inv % n)
    out_row = jnp.where(is_pad, 0, inv)
    tile_start = jnp.arange(n_tiles, dtype=I32) * tm
    used_start = jnp.minimum(tile_start, pend[-1] - tm)
    tile_onehot = ((used_start[:, None] >= pstart[None, :])
                   & (used_start[:, None] < pend[None, :])).astype(I32)
    tile_expert = jnp.sum(tile_onehot * jnp.arange(n_experts, dtype=I32)[None, :], axis=1)
    valid_end = jnp.sum(tile_onehot * (pstart + counts)[None, :], axis=1)
    tile_valid = jnp.where(tile_start < pend[-1], jnp.clip(valid_end - tile_start, 0, tm), 0)
    return (src_tok.reshape(n_tiles, 1, tm), out_row.reshape(n_tiles, 1, tm),
            tile_expert, n_used.reshape(1), tile_valid.astype(I32))


def _tiles(n_tokens, d_model):
    return dict(
        norm_rows=256,
        mm_tm=1024, mm_tn=512,
        gla_tc=512,
        moe_tm=256,
    )


def kernel(x, positions, norm_mix_g, w_in, gla_gate_w2, gla_gate_b, gla_out_norm_g,
           swa_sinks, w_out, norm_ffn_g, router_group_w, router_group_b,
           router_expert_w, router_expert_b, expert_w_gate, expert_w_up,
           expert_w_down, norm_final_g):
    bsz, t, d = x.shape
    n = bsz * t
    depth = w_in.shape[0]
    cfg = _tiles(n, d)

    rank = gla_gate_w2.shape[1]
    key_w = gla_gate_w2.shape[2]
    val_w = GLA_HEADS * gla_out_norm_g.shape[1]
    dk, dv = key_w // GLA_HEADS, val_w // GLA_HEADS
    q_heads = swa_sinks.shape[1]
    kv_heads = q_heads // SWA_GROUP
    sq_w, skv_w = q_heads * SWA_HEAD_DIM, kv_heads * SWA_HEAD_DIM
    n_experts = expert_w_gate.shape[1]
    tm = cfg["moe_tm"]
    p_rows = TOP_K * n + n_experts * tm

    c_gq, c_gk, c_gv, c_gr = 0, key_w, 2 * key_w, 2 * key_w + val_w
    c_ga = 2 * key_w + 2 * val_w
    c_sq = c_ga + rank
    m_sq = c_ga
    m_sk, m_sv = m_sq + sq_w, m_sq + sq_w + skv_w

    half = SWA_HEAD_DIM // 2
    inv_freq = ROPE_THETA ** (-jnp.arange(half, dtype=F32) / half)
    freq_lane = jnp.tile(inv_freq, LANES // half).reshape(1, LANES)
    posf = jnp.broadcast_to(positions.astype(F32).reshape(n, 1), (n, LANES))
    cos_t, sin_t = _rope_tables(posf, freq_lane, rows=1024)

    h2d = x.reshape(n, d)
    for l in range(depth):
        w_l = w_in[l]
        w_main = jnp.concatenate([w_l[:, :c_ga], w_l[:, c_sq:]], axis=1).astype(BF16)
        w_ga = jnp.pad(w_l[:, c_ga:c_sq], ((0, 0), (0, LANES - rank))).astype(BF16)
        w2p = jnp.pad(gla_gate_w2[l], ((0, LANES - rank), (0, 0))).astype(BF16)

        u, ga = _norm_gate(h2d, norm_mix_g[l].reshape(1, d), w_ga, rows=cfg["norm_rows"])
        proj = _matmul(u, w_main, tm=cfg["mm_tm"], tn=cfg["mm_tn"], out_dtype=BF16,
                       name="in_proj")
        proj3 = proj.reshape(bsz, t, -1)

        o_gla = _gla(proj3, ga.reshape(bsz, t, LANES), w2p,
                     gla_gate_b[l].reshape(1, key_w), gla_out_norm_g[l].reshape(1, dv),
                     heads=GLA_HEADS, dk=dk, dv=dv, tc=cfg["gla_tc"],
                     col_q=c_gq, col_k=c_gk, col_v=c_gv, col_r=c_gr)
        sinks_lane = jnp.repeat(swa_sinks[l], SWA_HEAD_DIM).reshape(1, sq_w)
        o_swa = _swa(proj3, cos_t, sin_t, sinks_lane, q_heads=q_heads, kv_heads=kv_heads,
                     col_q=m_sq, col_k=m_sk, col_v=m_sv)

        h2d = _out_proj(o_gla.reshape(n, val_w), o_swa.reshape(n, sq_w),
                        w_out[l].astype(BF16), h2d, tm=cfg["mm_tm"], tn=cfg["mm_tn"])

        wr = jnp.concatenate(
            [router_group_w[l],
             jnp.transpose(router_expert_w[l], (1, 0, 2)).reshape(d, n_experts)], axis=1)
        n_router = wr.shape[1]
        wr = jnp.pad(wr, ((0, 0), (0, LANES - n_router)))
        br = jnp.pad(jnp.concatenate([router_group_b[l], router_expert_b[l].reshape(-1)]),
                     (0, LANES - n_router)).reshape(1, LANES)
        u2, route = _router(h2d, norm_ffn_g[l].reshape(1, d), wr, br, rows=cfg["norm_rows"])

        eid = route[:, :TOP_K].astype(I32)
        src_tok, out_row, tile_expert, n_used, tile_valid = _dispatch_plan(
            eid, n_experts, tm, p_rows)
        y_tok = _experts(tile_expert, n_used, tile_valid, src_tok, out_row, u2,
                         expert_w_gate[l], expert_w_up[l], expert_w_down[l],
                         TOP_K * n, tm=tm)

        h2d = _combine(h2d, y_tok, route, norm_final_g.reshape(1, d),
                       rows=cfg["norm_rows"], final_norm=(l == depth - 1))
    return h2d.reshape(bsz, t, d)
```

```python
import functools

import jax
import jax.numpy as jnp
from jax import lax
from jax.experimental import pallas as pl
from jax.experimental.pallas import tpu as pltpu

F32 = jnp.float32
BF16 = jnp.bfloat16
I32 = jnp.int32

RMS_EPS = 1e-6

GLA_HEADS = 8
GLA_GATE_TAU = 16.0
GLA_CHUNK = 64
GLA_HEADS_PER_STEP = 4
SWA_HEAD_DIM = 64
SWA_GROUP = 8
SWA_BLOCK = 128
ROPE_THETA = 10000.0
N_GROUPS = 8
EXPERTS_PER_GROUP = 8
TOP_K = 2

LANES = 128
VMEM_LIMIT_BYTES = 56 * 1024 * 1024

NEG_BIG = -1e30


def _cparams(semantics):
    return pltpu.CompilerParams(dimension_semantics=semantics,
                                vmem_limit_bytes=VMEM_LIMIT_BYTES)


def _norm_gate_kernel(x_ref, g_ref, wga_ref, u_ref, ga_ref):
    x = x_ref[...]
    ms = jnp.mean(x * x, axis=-1, keepdims=True)
    u = (x * lax.rsqrt(ms + RMS_EPS) * g_ref[...]).astype(BF16)
    u_ref[...] = u
    ga_ref[...] = jnp.dot(u, wga_ref[...], preferred_element_type=F32)


def _norm_gate(x2d, g, wga, *, rows):
    n, d = x2d.shape
    return pl.pallas_call(
        _norm_gate_kernel,
        grid=(n // rows,),
        in_specs=[pl.BlockSpec((rows, d), lambda i: (i, 0)),
                  pl.BlockSpec((1, d), lambda i: (0, 0)),
                  pl.BlockSpec((d, LANES), lambda i: (0, 0))],
        out_specs=[pl.BlockSpec((rows, d), lambda i: (i, 0)),
                   pl.BlockSpec((rows, LANES), lambda i: (i, 0))],
        out_shape=[jax.ShapeDtypeStruct((n, d), BF16),
                   jax.ShapeDtypeStruct((n, LANES), F32)],
        compiler_params=_cparams(("parallel",)),
        name="norm_gate",
    )(x2d, g, wga)


def _matmul_kernel(a_ref, w_ref, o_ref):
    o_ref[...] = jnp.dot(a_ref[...], w_ref[...],
                         preferred_element_type=F32).astype(o_ref.dtype)


def _matmul(a, w, *, tm, tn, out_dtype, name):
    m, k = a.shape
    _, n = w.shape
    return pl.pallas_call(
        _matmul_kernel,
        grid=(m // tm, n // tn),
        in_specs=[pl.BlockSpec((tm, k), lambda i, j: (i, 0)),
                  pl.BlockSpec((k, tn), lambda i, j: (0, j))],
        out_specs=pl.BlockSpec((tm, tn), lambda i, j: (i, j)),
        out_shape=jax.ShapeDtypeStruct((m, n), out_dtype),
        compiler_params=_cparams(("parallel", "arbitrary")),
        name=name,
    )(a, w)


def _out_proj_kernel(a_ref, b_ref, wa_ref, wb_ref, x_ref, o_ref):
    acc = jnp.dot(a_ref[...], wa_ref[...], preferred_element_type=F32)
    acc = acc + jnp.dot(b_ref[...], wb_ref[...], preferred_element_type=F32)
    o_ref[...] = x_ref[...] + acc


def _out_proj(a, b, w, x2d, *, tm, tn):
    m, ka = a.shape
    _, kb = b.shape
    _, n = w.shape
    return pl.pallas_call(
        _out_proj_kernel,
        grid=(m // tm, n // tn),
        in_specs=[pl.BlockSpec((tm, ka), lambda i, j: (i, 0)),
                  pl.BlockSpec((tm, kb), lambda i, j: (i, 0)),
                  pl.BlockSpec((ka, tn), lambda i, j: (0, j)),
                  pl.BlockSpec((kb, tn), lambda i, j: (ka // kb, j)),
                  pl.BlockSpec((tm, tn), lambda i, j: (i, j))],
        out_specs=pl.BlockSpec((tm, tn), lambda i, j: (i, j)),
        out_shape=jax.ShapeDtypeStruct((m, n), F32),
        compiler_params=_cparams(("parallel", "arbitrary")),
        name="out_proj",
    )(a, b, w, w, x2d)


def _gla_kernel(q_ref, k_ref, v_ref, r_ref, ga_ref, w2_ref, gb_ref, gn_ref, tri_ref,
                o_ref, st_ref, *, chunk, dk, dv):
    @pl.when(pl.program_id(2) == 0)
    def _():
        st_ref[...] = jnp.zeros_like(st_ref)

    tc = q_ref.shape[1]
    n_chunks = tc // chunk
    heads = range(q_ref.shape[2] // dk)
    rows = lambda a, c: a[c * chunk:(c + 1) * chunk, :]
    last = lambda a, c: a[(c + 1) * chunk - 1:(c + 1) * chunk, :]
    contract_last = (((1,), (1,)), ((), ()))
    contract_first = (((0,), (0,)), ((), ()))
    key = lambda h: slice(h * dk, (h + 1) * dk)
    val = lambda h: slice(h * dv, (h + 1) * dv)

    tri = tri_ref[...]
    tri_b = tri.astype(BF16)
    ga = ga_ref[0].astype(BF16)
    k, v, b = [], [], []
    for h in heads:
        k.append(k_ref[0, :, key(h)].astype(F32))
        v.append(v_ref[0, :, val(h)])
        z = jnp.dot(ga, w2_ref[:, key(h)], preferred_element_type=F32) + gb_ref[:, key(h)]
        g = -(jnp.maximum(-z, 0.0) + jnp.log1p(jnp.exp(-jnp.abs(z)))) / GLA_GATE_TAU
        g_hi = g.astype(BF16)
        g_lo = (g - g_hi.astype(F32)).astype(BF16)
        b.append(jnp.dot(tri_b, g_hi, preferred_element_type=F32)
                 + jnp.dot(tri_b, g_lo, preferred_element_type=F32))
    q_in, o = [], []
    for h in heads:
        q = q_ref[0, :, key(h)].astype(F32) * (dk ** -0.5)
        q_in.append((q * jnp.exp(b[h])).astype(BF16))
        k_in = (k[h] * jnp.exp(-b[h])).astype(BF16)
        a = lax.dot_general(q_in[h], k_in, contract_last, preferred_element_type=F32)
        a = jnp.where(tri > 0.0, a, 0.0).astype(BF16)
        o.append(jnp.dot(a, v[h], preferred_element_type=F32))

    kv_t = [[lax.dot_general(
        rows(v[h], c),
        (rows(k[h], c) * jnp.exp(last(b[h], c) - rows(b[h], c))).astype(BF16),
        contract_first, preferred_element_type=F32) for c in range(n_chunks)] for h in heads]
    st_in = []
    for h in heads:
        st = st_ref[h]
        st_in.append([])
        for c in range(n_chunks):
            st_in[h].append(st.astype(BF16))
            st = st * jnp.exp(last(b[h], c)) + kv_t[h][c]
        st_ref[h] = st
    for h in heads:
        o_inter = [lax.dot_general(rows(q_in[h], c), st_in[h][c], contract_last,
                                   preferred_element_type=F32) for c in range(n_chunks)]
        oh = o[h] + jnp.concatenate(o_inter, axis=0)
        ms = jnp.mean(oh * oh, axis=-1, keepdims=True)
        oh = oh * lax.rsqrt(ms + RMS_EPS) * gn_ref[...]
        r = r_ref[0, :, val(h)].astype(F32)
        oh = oh * (r / (1.0 + jnp.exp(-r)))
        o_ref[0, :, val(h)] = oh.astype(o_ref.dtype)


def _gla(proj, ga, w2p, gb, gn, *, heads, dk, dv, tc, col_q, col_k, col_v, col_r):
    bsz, t, _ = proj.shape
    kern = functools.partial(_gla_kernel, chunk=GLA_CHUNK, dk=dk, dv=dv)
    idx = jnp.arange(tc, dtype=I32)
    tri = ((idx[:, None] // GLA_CHUNK == idx[None, :] // GLA_CHUNK)
           & (idx[None, :] <= idx[:, None])).astype(F32)
    hp = GLA_HEADS_PER_STEP
    kw, vw = hp * dk, hp * dv
    return pl.pallas_call(
        kern,
        grid=(bsz, heads // hp, t // tc),
        in_specs=[
            pl.BlockSpec((1, tc, kw), lambda b, h, i: (b, i, col_q // kw + h)),
            pl.BlockSpec((1, tc, kw), lambda b, h, i: (b, i, col_k // kw + h)),
            pl.BlockSpec((1, tc, vw), lambda b, h, i: (b, i, col_v // vw + h)),
            pl.BlockSpec((1, tc, vw), lambda b, h, i: (b, i, col_r // vw + h)),
            pl.BlockSpec((1, tc, LANES), lambda b, h, i: (b, i, 0)),
            pl.BlockSpec((LANES, kw), lambda b, h, i: (0, h)),
            pl.BlockSpec((1, kw), lambda b, h, i: (0, h)),
            pl.BlockSpec((1, dv), lambda b, h, i: (0, 0)),
            pl.BlockSpec((tc, tc), lambda b, h, i: (0, 0)),
        ],
        out_specs=pl.BlockSpec((1, tc, vw), lambda b, h, i: (b, i, h)),
        out_shape=jax.ShapeDtypeStruct((bsz, t, heads * dv), BF16),
        scratch_shapes=[pltpu.VMEM((hp, dv, dk), F32)],
        compiler_params=_cparams(("parallel", "parallel", "arbitrary")),
        name="gla",
    )(proj, proj, proj, proj, ga, w2p, gb, gn, tri)


def _rope_table_kernel(pos_ref, freq_ref, cos_ref, sin_ref):
    ang = pos_ref[...] * freq_ref[...]
    cos_ref[...] = jnp.cos(ang)
    sin_ref[...] = jnp.sin(ang)


def _rope_tables(posf, freq, *, rows):
    n = posf.shape[0]
    spec = pl.BlockSpec((rows, LANES), lambda i: (i, 0))
    return pl.pallas_call(
        _rope_table_kernel,
        grid=(n // rows,),
        in_specs=[spec, pl.BlockSpec((1, LANES), lambda i: (0, 0))],
        out_specs=[spec, spec],
        out_shape=[jax.ShapeDtypeStruct((n, LANES), F32)] * 2,
        compiler_params=_cparams(("parallel",)),
        name="rope_tables",
    )(posf, freq)


def _rotate_half(x, first_half):
    half = SWA_HEAD_DIM // 2
    width = x.shape[-1]
    fwd = pltpu.roll(x, width - half, 1)
    bwd = pltpu.roll(x, half, 1)
    return jnp.where(first_half, -fwd, bwd)


def _swa_kernel(q_ref, kp_ref, kc_ref, vp_ref, vc_ref, cp_ref, sp_ref, cc_ref,
                sc_ref, sink_ref, o_ref, *, kv_heads):
    w = SWA_BLOCK
    d = SWA_HEAD_DIM
    blk = pl.program_id(1)
    lane = lax.broadcasted_iota(I32, (w, LANES), 1)
    first_half = (lane % d) < (d // 2)
    low_head = lane < d

    def rope(x, cos, sin, fh):
        return x * cos + _rotate_half(x, fh) * sin

    cos_c, sin_c = cc_ref[...], sc_ref[...]
    cos_p, sin_p = cp_ref[...], sp_ref[...]

    kw = kv_heads * d
    lane_k = lax.broadcasted_iota(I32, (w, kw), 1)
    fh_k = (lane_k % d) < (d // 2)
    reps = kw // LANES
    tile = lambda t: jnp.concatenate([t] * reps, axis=1) if reps > 1 else t
    k_prev = rope(kp_ref[0].astype(F32), tile(cos_p), tile(sin_p), fh_k)
    k_cur = rope(kc_ref[0].astype(F32), tile(cos_c), tile(sin_c), fh_k)
    k_all = jnp.concatenate([k_prev, k_cur], axis=0)
    v_all = jnp.concatenate([vp_ref[0], vc_ref[0]], axis=0)

    pairs_per_kv = SWA_GROUP // 2
    m_rows = pairs_per_kv * w
    qr = lax.broadcasted_iota(I32, (m_rows, w), 0) % w
    kc = lax.broadcasted_iota(I32, (m_rows, w), 1)
    from_prev = kc > qr
    no_prev = from_prev & (blk == 0)
    lane_v = lax.broadcasted_iota(I32, (2 * w, LANES), 1)

    scale = d ** -0.5
    chains = []
    for g in range(kv_heads):
        kg = k_all[:, g * d:(g + 1) * d]
        vg = v_all[:, g * d:(g + 1) * d]
        zk = jnp.zeros_like(kg)
        zv = jnp.zeros_like(vg)
        k_lo = jnp.concatenate([kg, zk], axis=1).astype(BF16)
        k_hi = jnp.concatenate([zk, kg], axis=1).astype(BF16)
        v_lo = jnp.where(lane_v == d, 1.0, jnp.concatenate([vg, zv], axis=1))
        v_hi = jnp.where(lane_v == 0, 1.0, jnp.concatenate([zv, vg], axis=1))
        q_parts, sink_lo, sink_hi = [], [], []
        for p in range(pairs_per_kv):
            cb = g * pairs_per_kv + p
            q2 = q_ref[0, :, cb * LANES:(cb + 1) * LANES].astype(F32)
            q_parts.append((rope(q2, cos_c, sin_c, first_half) * scale).astype(BF16))
            sink2 = sink_ref[:, cb * LANES:(cb + 1) * LANES]
            for sel, dst in ((low_head, sink_lo), (~low_head, sink_hi)):
                s1 = jnp.max(jnp.where(sel[:1], sink2, NEG_BIG), axis=-1, keepdims=True)
                dst.append(jnp.broadcast_to(s1, (w, 1)))
        q_st = jnp.concatenate(q_parts, axis=0)
        chains.append((q_st, k_lo, v_lo, jnp.concatenate(sink_lo, axis=0), d))
        chains.append((q_st, k_hi, v_hi, jnp.concatenate(sink_hi, axis=0), 0))

    scores = [lax.dot_general(q_st, kk, (((1,), (1,)), ((), ())), preferred_element_type=F32)
              for q_st, kk, _, _, _ in chains]
    probs, row_max = [], []
    for s, (_, _, _, sink, _) in zip(scores, chains):
        s = jnp.where(from_prev, s[:, :w], s[:, w:])
        s = jnp.where(no_prev, NEG_BIG, s)
        m = jnp.maximum(jnp.max(s, axis=-1, keepdims=True), sink)
        p_un = jnp.exp(s - m)
        probs.append(jnp.concatenate([jnp.where(from_prev, p_un, 0.0),
                                      jnp.where(from_prev, 0.0, p_un)], axis=1).astype(BF16))
        row_max.append(m)
    pv = [jnp.dot(p2, vv, preferred_element_type=F32)
          for p2, (_, _, vv, _, _) in zip(probs, chains)]
    outs = []
    for o, m, (_, _, _, sink, sum_lane) in zip(pv, row_max, chains):
        denom = o[:, sum_lane:sum_lane + 1] + jnp.exp(sink - m)
        outs.append(o / denom)
    low_rows = lax.broadcasted_iota(I32, (m_rows, LANES), 1) < d
    for g in range(kv_heads):
        acc = jnp.where(low_rows, outs[2 * g], outs[2 * g + 1])
        for p in range(pairs_per_kv):
            cb = g * pairs_per_kv + p
            o_ref[0, :, cb * LANES:(cb + 1) * LANES] = acc[p * w:(p + 1) * w].astype(o_ref.dtype)


def _swa(proj, cos, sin, sinks_lane, *, q_heads, kv_heads, col_q, col_k, col_v):
    bsz, t, _ = proj.shape
    w = SWA_BLOCK
    qw = q_heads * SWA_HEAD_DIM
    kw = kv_heads * SWA_HEAD_DIM
    prev = lambda i: jnp.maximum(i - 1, 0)
    kern = functools.partial(_swa_kernel, kv_heads=kv_heads)
    tab_p = pl.BlockSpec((w, LANES), lambda b, i: (b * (t // w) + prev(i), 0))
    tab_c = pl.BlockSpec((w, LANES), lambda b, i: (b * (t // w) + i, 0))
    return pl.pallas_call(
        kern,
        grid=(bsz, t // w),
        in_specs=[
            pl.BlockSpec((1, w, qw), lambda b, i: (b, i, col_q // qw)),
            pl.BlockSpec((1, w, kw), lambda b, i: (b, prev(i), col_k // kw)),
            pl.BlockSpec((1, w, kw), lambda b, i: (b, i, col_k // kw)),
            pl.BlockSpec((1, w, kw), lambda b, i: (b, prev(i), col_v // kw)),
            pl.BlockSpec((1, w, kw), lambda b, i: (b, i, col_v // kw)),
            tab_p, tab_p, tab_c, tab_c,
            pl.BlockSpec((1, qw), lambda b, i: (0, 0)),
        ],
        out_specs=pl.BlockSpec((1, w, qw), lambda b, i: (b, i, 0)),
        out_shape=jax.ShapeDtypeStruct((bsz, t, qw), BF16),
        compiler_params=_cparams(("parallel", "arbitrary")),
        name="swa",
    )(proj, proj, proj, proj, proj, cos, sin, cos, sin, sinks_lane)


def _router_kernel(h_ref, g_ref, wr_ref, br_ref, u_ref, rt_ref):
    h = h_ref[...]
    ms = jnp.mean(h * h, axis=-1, keepdims=True)
    u = h * lax.rsqrt(ms + RMS_EPS) * g_ref[...]
    u_ref[...] = u
    u_hi = u.astype(BF16)
    u_lo = (u - u_hi.astype(F32)).astype(BF16)
    both = jnp.dot(u_hi, wr_ref[...], preferred_element_type=F32)
    logits = (both[:, :LANES] + both[:, LANES:]
              + jnp.dot(u_lo, wr_ref[:, :LANES], preferred_element_type=F32) + br_ref[...])
    lane = lax.broadcasted_iota(I32, logits.shape, 1)
    is_group = lane < N_GROUPS
    gl = jnp.where(is_group, logits, NEG_BIG)
    gmax = jnp.max(gl, axis=-1, keepdims=True)
    gidx = jnp.min(jnp.where(gl == gmax, lane, LANES), axis=-1, keepdims=True)
    gsum = jnp.sum(jnp.where(is_group, jnp.exp(gl - gmax), 0.0), axis=-1, keepdims=True)
    g_w = 1.0 / gsum
    lo = N_GROUPS + gidx * EXPERTS_PER_GROUP
    sel = (lane >= lo) & (lane < lo + EXPERTS_PER_GROUP)
    el = jnp.where(sel, logits, NEG_BIG)
    m1 = jnp.max(el, axis=-1, keepdims=True)
    i1 = jnp.min(jnp.where(el == m1, lane, LANES), axis=-1, keepdims=True)
    el2 = jnp.where(lane == i1, NEG_BIG, el)
    m2 = jnp.max(el2, axis=-1, keepdims=True)
    i2 = jnp.min(jnp.where(el2 == m2, lane, LANES), axis=-1, keepdims=True)
    t = jnp.exp(m2 - m1)
    w1 = g_w / (1.0 + t)
    w2 = g_w * t / (1.0 + t)
    e1 = (i1 - N_GROUPS).astype(F32)
    e2 = (i2 - N_GROUPS).astype(F32)
    rt_ref[...] = jnp.where(lane == 0, e1,
                  jnp.where(lane == 1, e2,
                  jnp.where(lane == 2, w1,
                  jnp.where(lane == 3, w2, 0.0))))


def _router(h2d, g, wr, br, *, rows):
    n, d = h2d.shape
    return pl.pallas_call(
        _router_kernel,
        grid=(n // rows,),
        in_specs=[pl.BlockSpec((rows, d), lambda i: (i, 0)),
                  pl.BlockSpec((1, d), lambda i: (0, 0)),
                  pl.BlockSpec((d, 2 * LANES), lambda i: (0, 0)),
                  pl.BlockSpec((1, LANES), lambda i: (0, 0))],
        out_specs=[pl.BlockSpec((rows, d), lambda i: (i, 0)),
                   pl.BlockSpec((rows, LANES), lambda i: (i, 0))],
        out_shape=[jax.ShapeDtypeStruct((n, d), F32),
                   jax.ShapeDtypeStruct((n, LANES), F32)],
        compiler_params=_cparams(("parallel",)),
        name="router",
    )(h2d, g, wr, br)


def _expert_kernel(te_ref, nu_ref, tv_ref, src_ref, nxt_ref, dst_ref, u_ref, wg_ref, wu_ref,
                   wd_ref, y_ref, xbuf, ybuf, gsem, ssem, *, tm):
    i = pl.program_id(0)
    part = pl.program_id(1)
    n_used = nu_ref[0]

    def gather(idx_ref, slot, n_valid):
        def issue(k, carry):
            t = idx_ref[0, 0, k]
            pltpu.make_async_copy(u_ref.at[pl.ds(t, 1), :],
                                  xbuf.at[slot, pl.ds(k, 1), :], gsem.at[slot]).start()
            return carry
        lax.fori_loop(0, n_valid, issue, 0)

    def scatter(slot, n_valid):
        def issue(k, carry):
            r = dst_ref[0, 0, k]
            pltpu.make_async_copy(ybuf.at[slot, pl.ds(k, 1), :],
                                  y_ref.at[pl.ds(r, 1), :], ssem.at[slot]).start()
            return carry
        lax.fori_loop(0, n_valid, issue, 0)

    def rows_wait(copy_of, n_valid):
        bit = tm
        while bit >= 1:
            @pl.when((n_valid & bit) != 0)
            def _(bit=bit):
                copy_of(bit).wait()
            bit //= 2

    def gather_wait(slot, n_valid):
        rows_wait(lambda n: pltpu.make_async_copy(
            u_ref.at[pl.ds(0, n), :], xbuf.at[slot, pl.ds(0, n), :], gsem.at[slot]), n_valid)

    def scatter_wait(slot, n_valid):
        rows_wait(lambda n: pltpu.make_async_copy(
            ybuf.at[slot, pl.ds(0, n), :], y_ref.at[pl.ds(0, n), :], ssem.at[slot]), n_valid)

    @pl.when(i < n_used)
    def _():
        slot = i % 2

        @pl.when(part == 0)
        def _():
            @pl.when(i == 0)
            def _():
                xbuf[...] = jnp.zeros_like(xbuf)
                gather(src_ref, 0, tv_ref[0])

            @pl.when(i + 1 < n_used)
            def _():
                gather(nxt_ref, 1 - slot, tv_ref[i + 1])

            gather_wait(slot, tv_ref[i])

            @pl.when(i >= 2)
            def _():
                scatter_wait(slot, tv_ref[i - 2])

        x = xbuf[slot].astype(BF16)
        a = jnp.dot(x, wg_ref[0].astype(BF16), preferred_element_type=F32)
        b = jnp.dot(x, wu_ref[0].astype(BF16), preferred_element_type=F32)
        hid = (a / (1.0 + jnp.exp(-a)) * b).astype(BF16)
        y = jnp.dot(hid, wd_ref[0].astype(BF16), preferred_element_type=F32)

        @pl.when(part == 0)
        def _():
            ybuf[slot] = y

        @pl.when(part == 1)
        def _():
            ybuf[slot] = ybuf[slot] + y
            scatter(slot, tv_ref[i])

            @pl.when(i == n_used - 1)
            def _():
                scatter_wait(slot, tv_ref[i])

                @pl.when(i >= 1)
                def _():
                    scatter_wait(1 - slot, tv_ref[i - 1])


def _experts(tile_expert, n_used, tile_valid, src_tok, out_row, u2, wg, wu, wd, n_out_rows,
             *, tm):
    n_tiles = src_tok.shape[0]
    _, d = u2.shape
    _, _, f = wg.shape
    kern = functools.partial(_expert_kernel, tm=tm)
    idx_spec = lambda off: pl.BlockSpec(
        (1, 1, tm), lambda i, p, te, nu, tv: (jnp.minimum(i + off, n_tiles - 1), 0, 0),
        memory_space=pltpu.SMEM)
    def half(i, p, nu):
        last = nu[0] - 1
        return jnp.where(i <= last, (i + p) % 2, (last + 1) % 2)
    fh = f // 2
    grid_spec = pltpu.PrefetchScalarGridSpec(
        num_scalar_prefetch=3,
        grid=(n_tiles, 2),
        in_specs=[idx_spec(0), idx_spec(1), idx_spec(0),
                  pl.BlockSpec(memory_space=pl.ANY),
                  pl.BlockSpec((1, d, fh), lambda i, p, te, nu, tv: (te[i], 0, half(i, p, nu))),
                  pl.BlockSpec((1, d, fh), lambda i, p, te, nu, tv: (te[i], 0, half(i, p, nu))),
                  pl.BlockSpec((1, fh, d), lambda i, p, te, nu, tv: (te[i], half(i, p, nu), 0))],
        out_specs=pl.BlockSpec(memory_space=pl.ANY),
        scratch_shapes=[pltpu.VMEM((2, tm, d), F32), pltpu.VMEM((2, tm, d), F32),
                        pltpu.SemaphoreType.DMA((2,)), pltpu.SemaphoreType.DMA((2,))],
    )
    return pl.pallas_call(
        kern,
        grid_spec=grid_spec,
        out_shape=jax.ShapeDtypeStruct((n_out_rows, d), F32),
        compiler_params=_cparams(("arbitrary", "arbitrary")),
        name="experts",
    )(tile_expert, n_used, tile_valid, src_tok, src_tok, out_row, u2, wg, wu, wd)


def _combine_kernel(h_ref, y0_ref, y1_ref, rt_ref, g_ref, o_ref, *, final_norm):
    rt = rt_ref[...]
    w0 = rt[:, 2:3]
    w1 = rt[:, 3:4]
    h = h_ref[...] + w0 * y0_ref[...].astype(F32) + w1 * y1_ref[...].astype(F32)
    if final_norm:
        ms = jnp.mean(h * h, axis=-1, keepdims=True)
        h = h * lax.rsqrt(ms + RMS_EPS) * g_ref[...]
    o_ref[...] = h


def _combine(h2d, y, route, g, *, rows, final_norm):
    n, d = h2d.shape
    nb = n // rows
    return pl.pallas_call(
        functools.partial(_combine_kernel, final_norm=final_norm),
        grid=(nb,),
        in_specs=[pl.BlockSpec((rows, d), lambda i: (i, 0)),
                  pl.BlockSpec((rows, d), lambda i: (i, 0)),
                  pl.BlockSpec((rows, d), lambda i: (nb + i, 0)),
                  pl.BlockSpec((rows, LANES), lambda i: (i, 0)),
                  pl.BlockSpec((1, d), lambda i: (0, 0))],
        out_specs=pl.BlockSpec((rows, d), lambda i: (i, 0)),
        out_shape=jax.ShapeDtypeStruct((n, d), F32),
        compiler_params=_cparams(("parallel",)),
        name="combine",
    )(h2d, y, y, route, g)


def _dispatch_plan(eid, n_experts, tm, p_rows):
    n = eid.shape[0]
    n_flat = TOP_K * n
    n_tiles = p_rows // tm
    e_flat = eid.T.reshape(-1)
    onehot = (e_flat[:, None] == jnp.arange(n_experts, dtype=I32)[None, :]).astype(I32)
    csum = jnp.cumsum(onehot, axis=0)
    counts = csum[-1]
    pcounts = ((counts + tm - 1) // tm) * tm
    pend = jnp.cumsum(pcounts)
    pstart = pend - pcounts
    dest = jnp.sum(onehot * (csum - 1 + pstart[None, :]), axis=1)
    n_used = (pend[-1] // tm).astype(I32)
    inv = jnp.full((p_rows,), n_flat, I32).at[dest].set(
        jnp.arange(n_flat, dtype=I32), unique_indices=True)
    is_pad = inv >= n_flat
    src_tok = jnp.where(is_pad, 0, inv % n)
    out_row = jnp.where(is_pad, 0, inv)
    tile_start = jnp.arange(n_tiles, dtype=I32) * tm
    used_start = jnp.minimum(tile_start, pend[-1] - tm)
    tile_onehot = ((used_start[:, None] >= pstart[None, :])
                   & (used_start[:, None] < pend[None, :])).astype(I32)
    tile_expert = jnp.sum(tile_onehot * jnp.arange(n_experts, dtype=I32)[None, :], axis=1)
    valid_end = jnp.sum(tile_onehot * (pstart + counts)[None, :], axis=1)
    tile_valid = jnp.where(tile_start < pend[-1], jnp.clip(valid_end - tile_start, 0, tm), 0)
    return (src_tok.reshape(n_tiles, 1, tm), out_row.reshape(n_tiles, 1, tm),
            tile_expert, n_used.reshape(1), tile_valid.astype(I32))


def _tiles(n_tokens, d_model):
    return dict(
        norm_rows=256,
        mm_tm=1024, mm_tn=512,
        gla_tc=512,
        moe_tm=256,
    )


def kernel(x, positions, norm_mix_g, w_in, gla_gate_w2, gla_gate_b, gla_out_norm_g,
           swa_sinks, w_out, norm_ffn_g, router_group_w, router_group_b,
           router_expert_w, router_expert_b, expert_w_gate, expert_w_up,
           expert_w_down, norm_final_g):
    bsz, t, d = x.shape
    n = bsz * t
    depth = w_in.shape[0]
    cfg = _tiles(n, d)

    rank = gla_gate_w2.shape[1]
    key_w = gla_gate_w2.shape[2]
    val_w = GLA_HEADS * gla_out_norm_g.shape[1]
    dk, dv = key_w // GLA_HEADS, val_w // GLA_HEADS
    q_heads = swa_sinks.shape[1]
    kv_heads = q_heads // SWA_GROUP
    sq_w, skv_w = q_heads * SWA_HEAD_DIM, kv_heads * SWA_HEAD_DIM
    n_experts = expert_w_gate.shape[1]
    tm = cfg["moe_tm"]
    p_rows = TOP_K * n + n_experts * tm

    c_gq, c_gk, c_gv, c_gr = 0, key_w, 2 * key_w, 2 * key_w + val_w
    c_ga = 2 * key_w + 2 * val_w
    c_sq = c_ga + rank
    m_sq = c_ga
    m_sk, m_sv = m_sq + sq_w, m_sq + sq_w + skv_w

    half = SWA_HEAD_DIM // 2
    inv_freq = ROPE_THETA ** (-jnp.arange(half, dtype=F32) / half)
    freq_lane = jnp.tile(inv_freq, LANES // half).reshape(1, LANES)
    posf = jnp.broadcast_to(positions.astype(F32).reshape(n, 1), (n, LANES))
    cos_t, sin_t = _rope_tables(posf, freq_lane, rows=1024)

    h2d = x.reshape(n, d)
    for l in range(depth):
        w_l = w_in[l]
        w_main = jnp.concatenate([w_l[:, :c_ga], w_l[:, c_sq:]], axis=1).astype(BF16)
        w_ga = jnp.pad(w_l[:, c_ga:c_sq], ((0, 0), (0, LANES - rank))).astype(BF16)
        w2p = jnp.pad(gla_gate_w2[l], ((0, LANES - rank), (0, 0))).astype(BF16)

        u, ga = _norm_gate(h2d, norm_mix_g[l].reshape(1, d), w_ga, rows=cfg["norm_rows"])
        proj = _matmul(u, w_main, tm=cfg["mm_tm"], tn=cfg["mm_tn"], out_dtype=BF16,
                       name="in_proj")
        proj3 = proj.reshape(bsz, t, -1)

        o_gla = _gla(proj3, ga.reshape(bsz, t, LANES), w2p,
                     gla_gate_b[l].reshape(1, key_w), gla_out_norm_g[l].reshape(1, dv),
                     heads=GLA_HEADS, dk=dk, dv=dv, tc=cfg["gla_tc"],
                     col_q=c_gq, col_k=c_gk, col_v=c_gv, col_r=c_gr)
        sinks_lane = jnp.repeat(swa_sinks[l], SWA_HEAD_DIM).reshape(1, sq_w)
        o_swa = _swa(proj3, cos_t, sin_t, sinks_lane, q_heads=q_heads, kv_heads=kv_heads,
                     col_q=m_sq, col_k=m_sk, col_v=m_sv)

        h2d = _out_proj(o_gla.reshape(n, val_w), o_swa.reshape(n, sq_w),
                        w_out[l].astype(BF16), h2d, tm=cfg["mm_tm"], tn=cfg["mm_tn"])

        wr = jnp.concatenate(
            [router_group_w[l],
             jnp.transpose(router_expert_w[l], (1, 0, 2)).reshape(d, n_experts)], axis=1)
        n_router = wr.shape[1]
        wr = jnp.pad(wr, ((0, 0), (0, LANES - n_router)))
        wr_hi = wr.astype(BF16)
        wr = jnp.concatenate([wr_hi, (wr - wr_hi.astype(F32)).astype(BF16)], axis=1)
        br = jnp.pad(jnp.concatenate([router_group_b[l], router_expert_b[l].reshape(-1)]),
                     (0, LANES - n_router)).reshape(1, LANES)
        u2, route = _router(h2d, norm_ffn_g[l].reshape(1, d), wr, br, rows=cfg["norm_rows"])

        eid = route[:, :TOP_K].astype(I32)
        src_tok, out_row, tile_expert, n_used, tile_valid = _dispatch_plan(
            eid, n_experts, tm, p_rows)
        y_tok = _experts(tile_expert, n_used, tile_valid, src_tok, out_row, u2,
                         expert_w_gate[l], expert_w_up[l], expert_w_down[l],
                         TOP_K * n, tm=tm)

        h2d = _combine(h2d, y_tok, route, norm_final_g.reshape(1, d),
                       rows=cfg["norm_rows"], final_norm=(l == depth - 1))
    return h2d.reshape(bsz, t, d)
```

```python
import functools

import jax
import jax.numpy as jnp
from jax import lax
from jax.experimental import pallas as pl
from jax.experimental.pallas import tpu as pltpu

F32 = jnp.float32
BF16 = jnp.bfloat16
I32 = jnp.int32

RMS_EPS = 1e-6

GLA_HEADS = 8
GLA_GATE_TAU = 16.0
GLA_CHUNK = 64
GLA_HEADS_PER_STEP = 4
SWA_HEAD_DIM = 64
SWA_GROUP = 8
SWA_BLOCK = 128
ROPE_THETA = 10000.0
N_GROUPS = 8
EXPERTS_PER_GROUP = 8
TOP_K = 2

LANES = 128
VMEM_LIMIT_BYTES = 56 * 1024 * 1024

NEG_BIG = -1e30


def _cparams(semantics):
    return pltpu.CompilerParams(dimension_semantics=semantics,
                                vmem_limit_bytes=VMEM_LIMIT_BYTES)


def _norm_gate_kernel(x_ref, g_ref, wga_ref, u_ref, ga_ref):
    x = x_ref[...]
    ms = jnp.mean(x * x, axis=-1, keepdims=True)
    u = (x * lax.rsqrt(ms + RMS_EPS) * g_ref[...]).astype(BF16)
    u_ref[...] = u
    ga_ref[...] = jnp.dot(u, wga_ref[...], preferred_element_type=F32)


def _norm_gate(x2d, g, wga, *, rows):
    n, d = x2d.shape
    return pl.pallas_call(
        _norm_gate_kernel,
        grid=(n // rows,),
        in_specs=[pl.BlockSpec((rows, d), lambda i: (i, 0)),
                  pl.BlockSpec((1, d), lambda i: (0, 0)),
                  pl.BlockSpec((d, LANES), lambda i: (0, 0))],
        out_specs=[pl.BlockSpec((rows, d), lambda i: (i, 0)),
                   pl.BlockSpec((rows, LANES), lambda i: (i, 0))],
        out_shape=[jax.ShapeDtypeStruct((n, d), BF16),
                   jax.ShapeDtypeStruct((n, LANES), F32)],
        compiler_params=_cparams(("parallel",)),
        name="norm_gate",
    )(x2d, g, wga)


def _in_proj_kernel(a_ref, wa_ref, wb_ref, o_ref, *, n_a):
    j = pl.program_id(1)

    @pl.when(j < n_a)
    def _():
        o_ref[...] = jnp.dot(a_ref[...], wa_ref[...],
                             preferred_element_type=F32).astype(o_ref.dtype)

    @pl.when(j >= n_a)
    def _():
        o_ref[...] = jnp.dot(a_ref[...], wb_ref[...],
                             preferred_element_type=F32).astype(o_ref.dtype)


def _in_proj(a, wa, wb, n_a_cols, *, tm, tn):
    m, k = a.shape
    n_a = n_a_cols // tn
    n_b = wb.shape[1] // tn
    return pl.pallas_call(
        functools.partial(_in_proj_kernel, n_a=n_a),
        grid=(m // tm, n_a + n_b),
        in_specs=[pl.BlockSpec((tm, k), lambda i, j: (i, 0)),
                  pl.BlockSpec((k, tn), lambda i, j: (0, jnp.minimum(j, n_a - 1))),
                  pl.BlockSpec((k, tn), lambda i, j: (0, jnp.maximum(j - n_a, 0)))],
        out_specs=pl.BlockSpec((tm, tn), lambda i, j: (i, j)),
        out_shape=jax.ShapeDtypeStruct((m, n_a_cols + wb.shape[1]), BF16),
        compiler_params=_cparams(("parallel", "arbitrary")),
        name="in_proj",
    )(a, wa, wb)


def _out_proj_kernel(a_ref, b_ref, wa_ref, wb_ref, x_ref, o_ref):
    acc = jnp.dot(a_ref[...], wa_ref[...], preferred_element_type=F32)
    acc = acc + jnp.dot(b_ref[...], wb_ref[...], preferred_element_type=F32)
    o_ref[...] = x_ref[...] + acc


def _out_proj(a, b, w, x2d, *, tm, tn):
    m, ka = a.shape
    _, kb = b.shape
    _, n = w.shape
    return pl.pallas_call(
        _out_proj_kernel,
        grid=(m // tm, n // tn),
        in_specs=[pl.BlockSpec((tm, ka), lambda i, j: (i, 0)),
                  pl.BlockSpec((tm, kb), lambda i, j: (i, 0)),
                  pl.BlockSpec((ka, tn), lambda i, j: (0, j)),
                  pl.BlockSpec((kb, tn), lambda i, j: (ka // kb, j)),
                  pl.BlockSpec((tm, tn), lambda i, j: (i, j))],
        out_specs=pl.BlockSpec((tm, tn), lambda i, j: (i, j)),
        out_shape=jax.ShapeDtypeStruct((m, n), F32),
        compiler_params=_cparams(("parallel", "arbitrary")),
        name="out_proj",
    )(a, b, w, w, x2d)


def _gla_kernel(q_ref, k_ref, v_ref, r_ref, ga_ref, w2_ref, gb_ref, gn_ref, tri_ref,
                o_ref, st_ref, *, chunk, dk, dv):
    @pl.when(pl.program_id(2) == 0)
    def _():
        st_ref[...] = jnp.zeros_like(st_ref)

    tc = q_ref.shape[1]
    n_chunks = tc // chunk
    heads = range(q_ref.shape[2] // dk)
    rows = lambda a, c: a[c * chunk:(c + 1) * chunk, :]
    last = lambda a, c: a[(c + 1) * chunk - 1:(c + 1) * chunk, :]
    contract_last = (((1,), (1,)), ((), ()))
    contract_first = (((0,), (0,)), ((), ()))
    key = lambda h: slice(h * dk, (h + 1) * dk)
    val = lambda h: slice(h * dv, (h + 1) * dv)

    tri = tri_ref[...]
    tri_b = tri.astype(BF16)
    ga = ga_ref[0].astype(BF16)
    k, v, b = [], [], []
    for h in heads:
        k.append(k_ref[0, :, key(h)].astype(F32))
        v.append(v_ref[0, :, val(h)])
        z = jnp.dot(ga, w2_ref[:, key(h)], preferred_element_type=F32) + gb_ref[:, key(h)]
        g = -(jnp.maximum(-z, 0.0) + jnp.log1p(jnp.exp(-jnp.abs(z)))) / GLA_GATE_TAU
        g_hi = g.astype(BF16)
        g_lo = (g - g_hi.astype(F32)).astype(BF16)
        b.append(jnp.dot(tri_b, g_hi, preferred_element_type=F32)
                 + jnp.dot(tri_b, g_lo, preferred_element_type=F32))
    q_in, o = [], []
    for h in heads:
        q = q_ref[0, :, key(h)].astype(F32) * (dk ** -0.5)
        q_in.append((q * jnp.exp(b[h])).astype(BF16))
        k_in = (k[h] * jnp.exp(-b[h])).astype(BF16)
        a = lax.dot_general(q_in[h], k_in, contract_last, preferred_element_type=F32)
        a = jnp.where(tri > 0.0, a, 0.0).astype(BF16)
        o.append(jnp.dot(a, v[h], preferred_element_type=F32))

    kv_t = [[lax.dot_general(
        rows(v[h], c),
        (rows(k[h], c) * jnp.exp(last(b[h], c) - rows(b[h], c))).astype(BF16),
        contract_first, preferred_element_type=F32) for c in range(n_chunks)] for h in heads]
    st_in = []
    for h in heads:
        st = st_ref[h]
        st_in.append([])
        for c in range(n_chunks):
            st_in[h].append(st.astype(BF16))
            st = st * jnp.exp(last(b[h], c)) + kv_t[h][c]
        st_ref[h] = st
    for h in heads:
        o_inter = [lax.dot_general(rows(q_in[h], c), st_in[h][c], contract_last,
                                   preferred_element_type=F32) for c in range(n_chunks)]
        oh = o[h] + jnp.concatenate(o_inter, axis=0)
        ms = jnp.mean(oh * oh, axis=-1, keepdims=True)
        oh = oh * lax.rsqrt(ms + RMS_EPS) * gn_ref[...]
        r = r_ref[0, :, val(h)].astype(F32)
        oh = oh * (r / (1.0 + jnp.exp(-r)))
        o_ref[0, :, val(h)] = oh.astype(o_ref.dtype)


def _gla(proj, ga, w2p, gb, gn, *, heads, dk, dv, tc, col_q, col_k, col_v, col_r):
    bsz, t, _ = proj.shape
    kern = functools.partial(_gla_kernel, chunk=GLA_CHUNK, dk=dk, dv=dv)
    idx = jnp.arange(tc, dtype=I32)
    tri = ((idx[:, None] // GLA_CHUNK == idx[None, :] // GLA_CHUNK)
           & (idx[None, :] <= idx[:, None])).astype(F32)
    hp = GLA_HEADS_PER_STEP
    kw, vw = hp * dk, hp * dv
    return pl.pallas_call(
        kern,
        grid=(bsz, heads // hp, t // tc),
        in_specs=[
            pl.BlockSpec((1, tc, kw), lambda b, h, i: (b, i, col_q // kw + h)),
            pl.BlockSpec((1, tc, kw), lambda b, h, i: (b, i, col_k // kw + h)),
            pl.BlockSpec((1, tc, vw), lambda b, h, i: (b, i, col_v // vw + h)),
            pl.BlockSpec((1, tc, vw), lambda b, h, i: (b, i, col_r // vw + h)),
            pl.BlockSpec((1, tc, LANES), lambda b, h, i: (b, i, 0)),
            pl.BlockSpec((LANES, kw), lambda b, h, i: (0, h)),
            pl.BlockSpec((1, kw), lambda b, h, i: (0, h)),
            pl.BlockSpec((1, dv), lambda b, h, i: (0, 0)),
            pl.BlockSpec((tc, tc), lambda b, h, i: (0, 0)),
        ],
        out_specs=pl.BlockSpec((1, tc, vw), lambda b, h, i: (b, i, h)),
        out_shape=jax.ShapeDtypeStruct((bsz, t, heads * dv), BF16),
        scratch_shapes=[pltpu.VMEM((hp, dv, dk), F32)],
        compiler_params=_cparams(("parallel", "parallel", "arbitrary")),
        name="gla",
    )(proj, proj, proj, proj, ga, w2p, gb, gn, tri)


def _rope_table_kernel(pos_ref, freq_ref, cos_ref, sin_ref):
    ang = pos_ref[...] * freq_ref[...]
    cos_ref[...] = jnp.cos(ang)
    sin_ref[...] = jnp.sin(ang)


def _rope_tables(posf, freq, *, rows):
    n = posf.shape[0]
    spec = pl.BlockSpec((rows, LANES), lambda i: (i, 0))
    return pl.pallas_call(
        _rope_table_kernel,
        grid=(n // rows,),
        in_specs=[spec, pl.BlockSpec((1, LANES), lambda i: (0, 0))],
        out_specs=[spec, spec],
        out_shape=[jax.ShapeDtypeStruct((n, LANES), F32)] * 2,
        compiler_params=_cparams(("parallel",)),
        name="rope_tables",
    )(posf, freq)


def _rotate_half(x, first_half):
    half = SWA_HEAD_DIM // 2
    width = x.shape[-1]
    fwd = pltpu.roll(x, width - half, 1)
    bwd = pltpu.roll(x, half, 1)
    return jnp.where(first_half, -fwd, bwd)


def _swa_kernel(q_ref, kp_ref, kc_ref, vp_ref, vc_ref, cp_ref, sp_ref, cc_ref,
                sc_ref, sink_ref, o_ref, *, kv_heads):
    w = SWA_BLOCK
    d = SWA_HEAD_DIM
    blk = pl.program_id(1)
    lane = lax.broadcasted_iota(I32, (w, LANES), 1)
    first_half = (lane % d) < (d // 2)
    low_head = lane < d

    def rope(x, cos, sin, fh):
        return x * cos + _rotate_half(x, fh) * sin

    cos_c, sin_c = cc_ref[...], sc_ref[...]
    cos_p, sin_p = cp_ref[...], sp_ref[...]

    kw = kv_heads * d
    lane_k = lax.broadcasted_iota(I32, (w, kw), 1)
    fh_k = (lane_k % d) < (d // 2)
    reps = kw // LANES
    tile = lambda t: jnp.concatenate([t] * reps, axis=1) if reps > 1 else t
    k_prev = rope(kp_ref[0].astype(F32), tile(cos_p), tile(sin_p), fh_k)
    k_cur = rope(kc_ref[0].astype(F32), tile(cos_c), tile(sin_c), fh_k)
    k_all = jnp.concatenate([k_prev, k_cur], axis=0)
    v_all = jnp.concatenate([vp_ref[0], vc_ref[0]], axis=0)

    pairs_per_kv = SWA_GROUP // 2
    m_rows = pairs_per_kv * w
    qr = lax.broadcasted_iota(I32, (m_rows, w), 0) % w
    kc = lax.broadcasted_iota(I32, (m_rows, w), 1)
    from_prev = kc > qr
    no_prev = from_prev & (blk == 0)
    lane_v = lax.broadcasted_iota(I32, (2 * w, LANES), 1)

    scale = d ** -0.5
    chains = []
    for g in range(kv_heads):
        kg = k_all[:, g * d:(g + 1) * d]
        vg = v_all[:, g * d:(g + 1) * d]
        zk = jnp.zeros_like(kg)
        zv = jnp.zeros_like(vg)
        k_lo = jnp.concatenate([kg, zk], axis=1).astype(BF16)
        k_hi = jnp.concatenate([zk, kg], axis=1).astype(BF16)
        v_lo = jnp.where(lane_v == d, 1.0, jnp.concatenate([vg, zv], axis=1))
        v_hi = jnp.where(lane_v == 0, 1.0, jnp.concatenate([zv, vg], axis=1))
        q_parts, sink_lo, sink_hi = [], [], []
        for p in range(pairs_per_kv):
            cb = g * pairs_per_kv + p
            q2 = q_ref[0, :, cb * LANES:(cb + 1) * LANES].astype(F32)
            q_parts.append((rope(q2, cos_c, sin_c, first_half) * scale).astype(BF16))
            sink2 = sink_ref[:, cb * LANES:(cb + 1) * LANES]
            for sel, dst in ((low_head, sink_lo), (~low_head, sink_hi)):
                s1 = jnp.max(jnp.where(sel[:1], sink2, NEG_BIG), axis=-1, keepdims=True)
                dst.append(jnp.broadcast_to(s1, (w, 1)))
        q_st = jnp.concatenate(q_parts, axis=0)
        chains.append((q_st, k_lo, v_lo, jnp.concatenate(sink_lo, axis=0), d))
        chains.append((q_st, k_hi, v_hi, jnp.concatenate(sink_hi, axis=0), 0))

    scores = [lax.dot_general(q_st, kk, (((1,), (1,)), ((), ())), preferred_element_type=F32)
              for q_st, kk, _, _, _ in chains]
    probs, row_max = [], []
    for s, (_, _, _, sink, _) in zip(scores, chains):
        s = jnp.where(from_prev, s[:, :w], s[:, w:])
        s = jnp.where(no_prev, NEG_BIG, s)
        m = jnp.maximum(jnp.max(s, axis=-1, keepdims=True), sink)
        p_un = jnp.exp(s - m)
        probs.append(jnp.concatenate([jnp.where(from_prev, p_un, 0.0),
                                      jnp.where(from_prev, 0.0, p_un)], axis=1).astype(BF16))
        row_max.append(m)
    pv = [jnp.dot(p2, vv, preferred_element_type=F32)
          for p2, (_, _, vv, _, _) in zip(probs, chains)]
    outs = []
    for o, m, (_, _, _, sink, sum_lane) in zip(pv, row_max, chains):
        denom = o[:, sum_lane:sum_lane + 1] + jnp.exp(sink - m)
        outs.append(o / denom)
    low_rows = lax.broadcasted_iota(I32, (m_rows, LANES), 1) < d
    for g in range(kv_heads):
        acc = jnp.where(low_rows, outs[2 * g], outs[2 * g + 1])
        for p in range(pairs_per_kv):
            cb = g * pairs_per_kv + p
            o_ref[0, :, cb * LANES:(cb + 1) * LANES] = acc[p * w:(p + 1) * w].astype(o_ref.dtype)


def _swa(proj, cos, sin, sinks_lane, *, q_heads, kv_heads, col_q, col_k, col_v):
    bsz, t, _ = proj.shape
    w = SWA_BLOCK
    qw = q_heads * SWA_HEAD_DIM
    kw = kv_heads * SWA_HEAD_DIM
    prev = lambda i: jnp.maximum(i - 1, 0)
    kern = functools.partial(_swa_kernel, kv_heads=kv_heads)
    tab_p = pl.BlockSpec((w, LANES), lambda b, i: (b * (t // w) + prev(i), 0))
    tab_c = pl.BlockSpec((w, LANES), lambda b, i: (b * (t // w) + i, 0))
    return pl.pallas_call(
        kern,
        grid=(bsz, t // w),
        in_specs=[
            pl.BlockSpec((1, w, qw), lambda b, i: (b, i, col_q // qw)),
            pl.BlockSpec((1, w, kw), lambda b, i: (b, prev(i), col_k // kw)),
            pl.BlockSpec((1, w, kw), lambda b, i: (b, i, col_k // kw)),
            pl.BlockSpec((1, w, kw), lambda b, i: (b, prev(i), col_v // kw)),
            pl.BlockSpec((1, w, kw), lambda b, i: (b, i, col_v // kw)),
            tab_p, tab_p, tab_c, tab_c,
            pl.BlockSpec((1, qw), lambda b, i: (0, 0)),
        ],
        out_specs=pl.BlockSpec((1, w, qw), lambda b, i: (b, i, 0)),
        out_shape=jax.ShapeDtypeStruct((bsz, t, qw), BF16),
        compiler_params=_cparams(("parallel", "arbitrary")),
        name="swa",
    )(proj, proj, proj, proj, proj, cos, sin, cos, sin, sinks_lane)


def _router_kernel(h_ref, g_ref, wr_ref, br_ref, u_ref, rt_ref):
    h = h_ref[...]
    ms = jnp.mean(h * h, axis=-1, keepdims=True)
    u = h * lax.rsqrt(ms + RMS_EPS) * g_ref[...]
    u_ref[...] = u
    u_hi = u.astype(BF16)
    u_lo = (u - u_hi.astype(F32)).astype(BF16)
    both = jnp.dot(u_hi, wr_ref[...], preferred_element_type=F32)
    logits = (both[:, :LANES] + both[:, LANES:]
              + jnp.dot(u_lo, wr_ref[:, :LANES], preferred_element_type=F32) + br_ref[...])
    lane = lax.broadcasted_iota(I32, logits.shape, 1)
    is_group = lane < N_GROUPS
    gl = jnp.where(is_group, logits, NEG_BIG)
    gmax = jnp.max(gl, axis=-1, keepdims=True)
    gidx = jnp.min(jnp.where(gl == gmax, lane, LANES), axis=-1, keepdims=True)
    gsum = jnp.sum(jnp.where(is_group, jnp.exp(gl - gmax), 0.0), axis=-1, keepdims=True)
    g_w = 1.0 / gsum
    lo = N_GROUPS + gidx * EXPERTS_PER_GROUP
    sel = (lane >= lo) & (lane < lo + EXPERTS_PER_GROUP)
    el = jnp.where(sel, logits, NEG_BIG)
    m1 = jnp.max(el, axis=-1, keepdims=True)
    i1 = jnp.min(jnp.where(el == m1, lane, LANES), axis=-1, keepdims=True)
    el2 = jnp.where(lane == i1, NEG_BIG, el)
    m2 = jnp.max(el2, axis=-1, keepdims=True)
    i2 = jnp.min(jnp.where(el2 == m2, lane, LANES), axis=-1, keepdims=True)
    t = jnp.exp(m2 - m1)
    w1 = g_w / (1.0 + t)
    w2 = g_w * t / (1.0 + t)
    e1 = (i1 - N_GROUPS).astype(F32)
    e2 = (i2 - N_GROUPS).astype(F32)
    rt_ref[...] = jnp.where(lane == 0, e1,
                  jnp.where(lane == 1, e2,
                  jnp.where(lane == 2, w1,
                  jnp.where(lane == 3, w2, 0.0))))


def _router(h2d, g, wr, br, *, rows):
    n, d = h2d.shape
    return pl.pallas_call(
        _router_kernel,
        grid=(n // rows,),
        in_specs=[pl.BlockSpec((rows, d), lambda i: (i, 0)),
                  pl.BlockSpec((1, d), lambda i: (0, 0)),
                  pl.BlockSpec((d, 2 * LANES), lambda i: (0, 0)),
                  pl.BlockSpec((1, LANES), lambda i: (0, 0))],
        out_specs=[pl.BlockSpec((rows, d), lambda i: (i, 0)),
                   pl.BlockSpec((rows, LANES), lambda i: (i, 0))],
        out_shape=[jax.ShapeDtypeStruct((n, d), F32),
                   jax.ShapeDtypeStruct((n, LANES), F32)],
        compiler_params=_cparams(("parallel",)),
        name="router",
    )(h2d, g, wr, br)


def _expert_kernel(te_ref, nu_ref, tv_ref, src_ref, nxt_ref, dst_ref, u_ref, wg_ref, wu_ref,
                   wd_ref, y_ref, xbuf, ybuf, gsem, ssem, *, tm):
    i = pl.program_id(0)
    part = pl.program_id(1)
    n_used = nu_ref[0]

    def gather(idx_ref, slot, n_valid):
        def issue(k, carry):
            t = idx_ref[0, 0, k]
            pltpu.make_async_copy(u_ref.at[pl.ds(t, 1), :],
                                  xbuf.at[slot, pl.ds(k, 1), :], gsem.at[slot]).start()
            return carry
        lax.fori_loop(0, n_valid, issue, 0)

    def scatter(slot, n_valid):
        def issue(k, carry):
            r = dst_ref[0, 0, k]
            pltpu.make_async_copy(ybuf.at[slot, pl.ds(k, 1), :],
                                  y_ref.at[pl.ds(r, 1), :], ssem.at[slot]).start()
            return carry
        lax.fori_loop(0, n_valid, issue, 0)

    def rows_wait(copy_of, n_valid):
        bit = tm
        while bit >= 1:
            @pl.when((n_valid & bit) != 0)
            def _(bit=bit):
                copy_of(bit).wait()
            bit //= 2

    def gather_wait(slot, n_valid):
        rows_wait(lambda n: pltpu.make_async_copy(
            u_ref.at[pl.ds(0, n), :], xbuf.at[slot, pl.ds(0, n), :], gsem.at[slot]), n_valid)

    def scatter_wait(slot, n_valid):
        rows_wait(lambda n: pltpu.make_async_copy(
            ybuf.at[slot, pl.ds(0, n), :], y_ref.at[pl.ds(0, n), :], ssem.at[slot]), n_valid)

    @pl.when(i < n_used)
    def _():
        slot = i % 2

        @pl.when(part == 0)
        def _():
            @pl.when(i == 0)
            def _():
                xbuf[...] = jnp.zeros_like(xbuf)
                gather(src_ref, 0, tv_ref[0])

            @pl.when(i + 1 < n_used)
            def _():
                gather(nxt_ref, 1 - slot, tv_ref[i + 1])

            gather_wait(slot, tv_ref[i])

            @pl.when(i >= 2)
            def _():
                scatter_wait(slot, tv_ref[i - 2])

        x = xbuf[slot].astype(BF16)
        a = jnp.dot(x, wg_ref[0].astype(BF16), preferred_element_type=F32)
        b = jnp.dot(x, wu_ref[0].astype(BF16), preferred_element_type=F32)
        hid = (a / (1.0 + jnp.exp(-a)) * b).astype(BF16)
        y = jnp.dot(hid, wd_ref[0].astype(BF16), preferred_element_type=F32)

        @pl.when(part == 0)
        def _():
            ybuf[slot] = y

        @pl.when(part == 1)
        def _():
            ybuf[slot] = ybuf[slot] + y
            scatter(slot, tv_ref[i])

            @pl.when(i == n_used - 1)
            def _():
                scatter_wait(slot, tv_ref[i])

                @pl.when(i >= 1)
                def _():
                    scatter_wait(1 - slot, tv_ref[i - 1])


def _experts(tile_expert, n_used, tile_valid, src_tok, out_row, u2, wg, wu, wd, n_out_rows,
             *, tm):
    n_tiles = src_tok.shape[0]
    _, d = u2.shape
    _, _, f = wg.shape
    kern = functools.partial(_expert_kernel, tm=tm)
    idx_spec = lambda off: pl.BlockSpec(
        (1, 1, tm), lambda i, p, te, nu, tv: (jnp.minimum(i + off, n_tiles - 1), 0, 0),
        memory_space=pltpu.SMEM)
    def half(i, p, nu):
        last = nu[0] - 1
        return jnp.where(i <= last, (i + p) % 2, (last + 1) % 2)
    fh = f // 2
    grid_spec = pltpu.PrefetchScalarGridSpec(
        num_scalar_prefetch=3,
        grid=(n_tiles, 2),
        in_specs=[idx_spec(0), idx_spec(1), idx_spec(0),
                  pl.BlockSpec(memory_space=pl.ANY),
                  pl.BlockSpec((1, d, fh), lambda i, p, te, nu, tv: (te[i], 0, half(i, p, nu))),
                  pl.BlockSpec((1, d, fh), lambda i, p, te, nu, tv: (te[i], 0, half(i, p, nu))),
                  pl.BlockSpec((1, fh, d), lambda i, p, te, nu, tv: (te[i], half(i, p, nu), 0))],
        out_specs=pl.BlockSpec(memory_space=pl.ANY),
        scratch_shapes=[pltpu.VMEM((2, tm, d), F32), pltpu.VMEM((2, tm, d), F32),
                        pltpu.SemaphoreType.DMA((2,)), pltpu.SemaphoreType.DMA((2,))],
    )
    return pl.pallas_call(
        kern,
        grid_spec=grid_spec,
        out_shape=jax.ShapeDtypeStruct((n_out_rows, d), F32),
        compiler_params=_cparams(("arbitrary", "arbitrary")),
        name="experts",
    )(tile_expert, n_used, tile_valid, src_tok, src_tok, out_row, u2, wg, wu, wd)


def _combine_kernel(h_ref, y0_ref, y1_ref, rt_ref, g_ref, o_ref, *, final_norm):
    rt = rt_ref[...]
    w0 = rt[:, 2:3]
    w1 = rt[:, 3:4]
    h = h_ref[...] + w0 * y0_ref[...] + w1 * y1_ref[...]
    if final_norm:
        ms = jnp.mean(h * h, axis=-1, keepdims=True)
        h = h * lax.rsqrt(ms + RMS_EPS) * g_ref[...]
    o_ref[...] = h


def _combine(h2d, y, route, g, *, rows, final_norm):
    n, d = h2d.shape
    nb = n // rows
    return pl.pallas_call(
        functools.partial(_combine_kernel, final_norm=final_norm),
        grid=(nb,),
        in_specs=[pl.BlockSpec((rows, d), lambda i: (i, 0)),
                  pl.BlockSpec((rows, d), lambda i: (i, 0)),
                  pl.BlockSpec((rows, d), lambda i: (nb + i, 0)),
                  pl.BlockSpec((rows, LANES), lambda i: (i, 0)),
                  pl.BlockSpec((1, d), lambda i: (0, 0))],
        out_specs=pl.BlockSpec((rows, d), lambda i: (i, 0)),
        out_shape=jax.ShapeDtypeStruct((n, d), F32),
        compiler_params=_cparams(("parallel",)),
        name="combine",
    )(h2d, y, y, route, g)


def _dispatch_plan(eid, n_experts, tm, p_rows):
    n = eid.shape[0]
    n_flat = TOP_K * n
    n_tiles = p_rows // tm
    e_flat = eid.T.reshape(-1)
    onehot = (e_flat[:, None] == jnp.arange(n_experts, dtype=I32)[None, :]).astype(I32)
    csum = jnp.cumsum(onehot, axis=0)
    counts = csum[-1]
    pcounts = ((counts + tm - 1) // tm) * tm
    pend = jnp.cumsum(pcounts)
    pstart = pend - pcounts
    dest = jnp.sum(onehot * (csum - 1 + pstart[None, :]), axis=1)
    n_used = (pend[-1] // tm).astype(I32)
    inv = jnp.full((p_rows,), n_flat, I32).at[dest].set(
        jnp.arange(n_flat, dtype=I32), unique_indices=True)
    is_pad = inv >= n_flat
    src_tok = jnp.where(is_pad, 0, inv % n)
    out_row = jnp.where(is_pad, 0, inv)
    tile_start = jnp.arange(n_tiles, dtype=I32) * tm
    used_start = jnp.minimum(tile_start, pend[-1] - tm)
    tile_onehot = ((used_start[:, None] >= pstart[None, :])
                   & (used_start[:, None] < pend[None, :])).astype(I32)
    tile_expert = jnp.sum(tile_onehot * jnp.arange(n_experts, dtype=I32)[None, :], axis=1)
    valid_end = jnp.sum(tile_onehot * (pstart + counts)[None, :], axis=1)
    tile_valid = jnp.where(tile_start < pend[-1], jnp.clip(valid_end - tile_start, 0, tm), 0)
    return (src_tok.reshape(n_tiles, 1, tm), out_row.reshape(n_tiles, 1, tm),
            tile_expert, n_used.reshape(1), tile_valid.astype(I32))


def _tiles(n_tokens, d_model):
    return dict(
        norm_rows=256,
        in_tm=2048, in_tn=512,
        out_tm=1024, out_tn=1024,
        gla_tc=512,
        moe_tm=256,
    )


def kernel(x, positions, norm_mix_g, w_in, gla_gate_w2, gla_gate_b, gla_out_norm_g,
           swa_sinks, w_out, norm_ffn_g, router_group_w, router_group_b,
           router_expert_w, router_expert_b, expert_w_gate, expert_w_up,
           expert_w_down, norm_final_g):
    bsz, t, d = x.shape
    n = bsz * t
    depth = w_in.shape[0]
    cfg = _tiles(n, d)

    rank = gla_gate_w2.shape[1]
    key_w = gla_gate_w2.shape[2]
    val_w = GLA_HEADS * gla_out_norm_g.shape[1]
    dk, dv = key_w // GLA_HEADS, val_w // GLA_HEADS
    q_heads = swa_sinks.shape[1]
    kv_heads = q_heads // SWA_GROUP
    sq_w, skv_w = q_heads * SWA_HEAD_DIM, kv_heads * SWA_HEAD_DIM
    n_experts = expert_w_gate.shape[1]
    tm = cfg["moe_tm"]
    p_rows = TOP_K * n + n_experts * tm

    c_gq, c_gk, c_gv, c_gr = 0, key_w, 2 * key_w, 2 * key_w + val_w
    c_ga = 2 * key_w + 2 * val_w
    c_sq = c_ga + rank
    m_sq = c_ga
    m_sk, m_sv = m_sq + sq_w, m_sq + sq_w + skv_w

    half = SWA_HEAD_DIM // 2
    inv_freq = ROPE_THETA ** (-jnp.arange(half, dtype=F32) / half)
    freq_lane = jnp.tile(inv_freq, LANES // half).reshape(1, LANES)
    posf = jnp.broadcast_to(positions.astype(F32).reshape(n, 1), (n, LANES))
    cos_t, sin_t = _rope_tables(posf, freq_lane, rows=1024)

    h2d = x.reshape(n, d)
    for l in range(depth):
        w_bf = w_in[l].astype(BF16)
        w_swa = w_bf[:, c_sq:]
        w_ga = jnp.pad(w_bf[:, c_ga:c_sq], ((0, 0), (0, LANES - rank)))
        w2p = jnp.pad(gla_gate_w2[l], ((0, LANES - rank), (0, 0))).astype(BF16)

        u, ga = _norm_gate(h2d, norm_mix_g[l].reshape(1, d), w_ga, rows=cfg["norm_rows"])
        proj = _in_proj(u, w_bf, w_swa, c_ga, tm=cfg["in_tm"], tn=cfg["in_tn"])
        proj3 = proj.reshape(bsz, t, -1)

        o_gla = _gla(proj3, ga.reshape(bsz, t, LANES), w2p,
                     gla_gate_b[l].reshape(1, key_w), gla_out_norm_g[l].reshape(1, dv),
                     heads=GLA_HEADS, dk=dk, dv=dv, tc=cfg["gla_tc"],
                     col_q=c_gq, col_k=c_gk, col_v=c_gv, col_r=c_gr)
        sinks_lane = jnp.repeat(swa_sinks[l], SWA_HEAD_DIM).reshape(1, sq_w)
        o_swa = _swa(proj3, cos_t, sin_t, sinks_lane, q_heads=q_heads, kv_heads=kv_heads,
                     col_q=m_sq, col_k=m_sk, col_v=m_sv)

        h2d = _out_proj(o_gla.reshape(n, val_w), o_swa.reshape(n, sq_w),
                        w_out[l].astype(BF16), h2d, tm=cfg["out_tm"], tn=cfg["out_tn"])

        wr = jnp.concatenate(
            [router_group_w[l],
             jnp.transpose(router_expert_w[l], (1, 0, 2)).reshape(d, n_experts)], axis=1)
        n_router = wr.shape[1]
        wr = jnp.pad(wr, ((0, 0), (0, LANES - n_router)))
        wr_hi = wr.astype(BF16)
        wr = jnp.concatenate([wr_hi, (wr - wr_hi.astype(F32)).astype(BF16)], axis=1)
        br = jnp.pad(jnp.concatenate([router_group_b[l], router_expert_b[l].reshape(-1)]),
                     (0, LANES - n_router)).reshape(1, LANES)
        u2, route = _router(h2d, norm_ffn_g[l].reshape(1, d), wr, br, rows=cfg["norm_rows"])

        eid = route[:, :TOP_K].astype(I32)
        src_tok, out_row, tile_expert, n_used, tile_valid = _dispatch_plan(
            eid, n_experts, tm, p_rows)
        y_tok = _experts(tile_expert, n_used, tile_valid, src_tok, out_row, u2,
                         expert_w_gate[l], expert_w_up[l], expert_w_down[l],
                         TOP_K * n, tm=tm)

        h2d = _combine(h2d, y_tok, route, norm_final_g.reshape(1, d),
                       rows=cfg["norm_rows"], final_norm=(l == depth - 1))
    return h2d.reshape(bsz, t, d)
```

```python
import functools

import jax
import jax.numpy as jnp
from jax import lax
from jax.experimental import pallas as pl
from jax.experimental.pallas import tpu as pltpu

F32 = jnp.float32
BF16 = jnp.bfloat16
I32 = jnp.int32

RMS_EPS = 1e-6

GLA_HEADS = 8
GLA_GATE_TAU = 16.0
GLA_CHUNK = 64
GLA_HEADS_PER_STEP = 8
SWA_HEAD_DIM = 64
SWA_GROUP = 8
SWA_BLOCK = 128
ROPE_THETA = 10000.0
N_GROUPS = 8
EXPERTS_PER_GROUP = 8
TOP_K = 2

LANES = 128
VMEM_LIMIT_BYTES = 56 * 1024 * 1024

NEG_BIG = -1e30


def _cparams(semantics):
    return pltpu.CompilerParams(dimension_semantics=semantics,
                                vmem_limit_bytes=VMEM_LIMIT_BYTES)


def _norm_gate_kernel(x_ref, g_ref, wga_ref, u_ref, ga_ref):
    x = x_ref[...]
    ms = jnp.mean(x * x, axis=-1, keepdims=True)
    u = (x * lax.rsqrt(ms + RMS_EPS) * g_ref[...]).astype(BF16)
    u_ref[...] = u
    ga_ref[...] = jnp.dot(u, wga_ref[...], preferred_element_type=F32)


def _norm_gate(x2d, g, wga, *, rows):
    n, d = x2d.shape
    return pl.pallas_call(
        _norm_gate_kernel,
        grid=(n // rows,),
        in_specs=[pl.BlockSpec((rows, d), lambda i: (i, 0)),
                  pl.BlockSpec((1, d), lambda i: (0, 0)),
                  pl.BlockSpec((d, LANES), lambda i: (0, 0))],
        out_specs=[pl.BlockSpec((rows, d), lambda i: (i, 0)),
                   pl.BlockSpec((rows, LANES), lambda i: (i, 0))],
        out_shape=[jax.ShapeDtypeStruct((n, d), BF16),
                   jax.ShapeDtypeStruct((n, LANES), F32)],
        compiler_params=_cparams(("parallel",)),
        name="norm_gate",
    )(x2d, g, wga)


def _in_proj_kernel(a_ref, wa_ref, wb_ref, o_ref, *, n_a):
    j = pl.program_id(1)

    @pl.when(j < n_a)
    def _():
        o_ref[...] = jnp.dot(a_ref[...], wa_ref[...],
                             preferred_element_type=F32).astype(o_ref.dtype)

    @pl.when(j >= n_a)
    def _():
        o_ref[...] = jnp.dot(a_ref[...], wb_ref[...],
                             preferred_element_type=F32).astype(o_ref.dtype)


def _in_proj(a, wa, wb, n_a_cols, *, tm, tn):
    m, k = a.shape
    n_a = n_a_cols // tn
    n_b = wb.shape[1] // tn
    return pl.pallas_call(
        functools.partial(_in_proj_kernel, n_a=n_a),
        grid=(m // tm, n_a + n_b),
        in_specs=[pl.BlockSpec((tm, k), lambda i, j: (i, 0)),
                  pl.BlockSpec((k, tn), lambda i, j: (0, jnp.minimum(j, n_a - 1))),
                  pl.BlockSpec((k, tn), lambda i, j: (0, jnp.maximum(j - n_a, 0)))],
        out_specs=pl.BlockSpec((tm, tn), lambda i, j: (i, j)),
        out_shape=jax.ShapeDtypeStruct((m, n_a_cols + wb.shape[1]), BF16),
        compiler_params=_cparams(("parallel", "arbitrary")),
        name="in_proj",
    )(a, wa, wb)


def _out_proj_kernel(a_ref, b_ref, wa_ref, wb_ref, x_ref, o_ref):
    acc = jnp.dot(a_ref[...], wa_ref[...], preferred_element_type=F32)
    acc = acc + jnp.dot(b_ref[...], wb_ref[...], preferred_element_type=F32)
    o_ref[...] = x_ref[...] + acc


def _out_proj(a, b, w, x2d, *, tm, tn):
    m, ka = a.shape
    _, kb = b.shape
    _, n = w.shape
    return pl.pallas_call(
        _out_proj_kernel,
        grid=(m // tm, n // tn),
        in_specs=[pl.BlockSpec((tm, ka), lambda i, j: (i, 0)),
                  pl.BlockSpec((tm, kb), lambda i, j: (i, 0)),
                  pl.BlockSpec((ka, tn), lambda i, j: (0, j)),
                  pl.BlockSpec((kb, tn), lambda i, j: (ka // kb, j)),
                  pl.BlockSpec((tm, tn), lambda i, j: (i, j))],
        out_specs=pl.BlockSpec((tm, tn), lambda i, j: (i, j)),
        out_shape=jax.ShapeDtypeStruct((m, n), F32),
        compiler_params=_cparams(("parallel", "arbitrary")),
        name="out_proj",
    )(a, b, w, w, x2d)


def _gla_kernel(q_ref, k_ref, v_ref, r_ref, ga_ref, w2_ref, gb_ref, gn_ref, tri_ref,
                o_ref, st_ref, *, chunk, dk, dv):
    @pl.when(pl.program_id(2) == 0)
    def _():
        st_ref[...] = jnp.zeros_like(st_ref)

    tc = q_ref.shape[1]
    n_chunks = tc // chunk
    heads = range(q_ref.shape[2] // dk)
    rows = lambda a, c: a[c * chunk:(c + 1) * chunk, :]
    last = lambda a, c: a[(c + 1) * chunk - 1:(c + 1) * chunk, :]
    contract_last = (((1,), (1,)), ((), ()))
    contract_first = (((0,), (0,)), ((), ()))
    key = lambda h: slice(h * dk, (h + 1) * dk)
    val = lambda h: slice(h * dv, (h + 1) * dv)

    tri = tri_ref[...]
    tri_b = tri.astype(BF16)
    ga = ga_ref[0].astype(BF16)
    k, v, b = [], [], []
    for h in heads:
        k.append(k_ref[0, :, key(h)].astype(F32))
        v.append(v_ref[0, :, val(h)])
        z = jnp.dot(ga, w2_ref[:, key(h)], preferred_element_type=F32) + gb_ref[:, key(h)]
        g = -(jnp.maximum(-z, 0.0) + jnp.log1p(jnp.exp(-jnp.abs(z)))) / GLA_GATE_TAU
        g_hi = g.astype(BF16)
        g_lo = (g - g_hi.astype(F32)).astype(BF16)
        b.append(jnp.dot(tri_b, g_hi, preferred_element_type=F32)
                 + jnp.dot(tri_b, g_lo, preferred_element_type=F32))
    q_in, o = [], []
    for h in heads:
        q = q_ref[0, :, key(h)].astype(F32) * (dk ** -0.5)
        q_in.append((q * jnp.exp(b[h])).astype(BF16))
        k_in = (k[h] * jnp.exp(-b[h])).astype(BF16)
        a = lax.dot_general(q_in[h], k_in, contract_last, preferred_element_type=F32)
        a = jnp.where(tri > 0.0, a, 0.0).astype(BF16)
        o.append(jnp.dot(a, v[h], preferred_element_type=F32))

    kv_t = [[lax.dot_general(
        rows(v[h], c),
        (rows(k[h], c) * jnp.exp(last(b[h], c) - rows(b[h], c))).astype(BF16),
        contract_first, preferred_element_type=F32) for c in range(n_chunks)] for h in heads]
    st_in = []
    for h in heads:
        st = st_ref[h]
        st_in.append([])
        for c in range(n_chunks):
            st_in[h].append(st.astype(BF16))
            st = st * jnp.exp(last(b[h], c)) + kv_t[h][c]
        st_ref[h] = st
    for h in heads:
        o_inter = [lax.dot_general(rows(q_in[h], c), st_in[h][c], contract_last,
                                   preferred_element_type=F32) for c in range(n_chunks)]
        oh = o[h] + jnp.concatenate(o_inter, axis=0)
        ms = jnp.mean(oh * oh, axis=-1, keepdims=True)
        oh = oh * lax.rsqrt(ms + RMS_EPS) * gn_ref[...]
        r = r_ref[0, :, val(h)].astype(F32)
        oh = oh * (r / (1.0 + jnp.exp(-r)))
        o_ref[0, :, val(h)] = oh.astype(o_ref.dtype)


def _gla(proj, ga, w2p, gb, gn, *, heads, dk, dv, tc, col_q, col_k, col_v, col_r):
    bsz, t, _ = proj.shape
    kern = functools.partial(_gla_kernel, chunk=GLA_CHUNK, dk=dk, dv=dv)
    idx = jnp.arange(tc, dtype=I32)
    tri = ((idx[:, None] // GLA_CHUNK == idx[None, :] // GLA_CHUNK)
           & (idx[None, :] <= idx[:, None])).astype(F32)
    hp = GLA_HEADS_PER_STEP
    kw, vw = hp * dk, hp * dv
    return pl.pallas_call(
        kern,
        grid=(bsz, heads // hp, t // tc),
        in_specs=[
            pl.BlockSpec((1, tc, kw), lambda b, h, i: (b, i, col_q // kw + h)),
            pl.BlockSpec((1, tc, kw), lambda b, h, i: (b, i, col_k // kw + h)),
            pl.BlockSpec((1, tc, vw), lambda b, h, i: (b, i, col_v // vw + h)),
            pl.BlockSpec((1, tc, vw), lambda b, h, i: (b, i, col_r // vw + h)),
            pl.BlockSpec((1, tc, LANES), lambda b, h, i: (b, i, 0)),
            pl.BlockSpec((LANES, kw), lambda b, h, i: (0, h)),
            pl.BlockSpec((1, kw), lambda b, h, i: (0, h)),
            pl.BlockSpec((1, dv), lambda b, h, i: (0, 0)),
            pl.BlockSpec((tc, tc), lambda b, h, i: (0, 0)),
        ],
        out_specs=pl.BlockSpec((1, tc, vw), lambda b, h, i: (b, i, h)),
        out_shape=jax.ShapeDtypeStruct((bsz, t, heads * dv), BF16),
        scratch_shapes=[pltpu.VMEM((hp, dv, dk), F32)],
        compiler_params=_cparams(("parallel", "parallel", "arbitrary")),
        name="gla",
    )(proj, proj, proj, proj, ga, w2p, gb, gn, tri)


def _rope_table_kernel(pos_ref, freq_ref, cos_ref, sin_ref):
    ang = pos_ref[...] * freq_ref[...]
    cos_ref[...] = jnp.cos(ang)
    sin_ref[...] = jnp.sin(ang)


def _rope_tables(posf, freq, *, rows):
    n = posf.shape[0]
    spec = pl.BlockSpec((rows, LANES), lambda i: (i, 0))
    return pl.pallas_call(
        _rope_table_kernel,
        grid=(n // rows,),
        in_specs=[spec, pl.BlockSpec((1, LANES), lambda i: (0, 0))],
        out_specs=[spec, spec],
        out_shape=[jax.ShapeDtypeStruct((n, LANES), F32)] * 2,
        compiler_params=_cparams(("parallel",)),
        name="rope_tables",
    )(posf, freq)


def _rotate_half(x, first_half):
    half = SWA_HEAD_DIM // 2
    width = x.shape[-1]
    fwd = pltpu.roll(x, width - half, 1)
    bwd = pltpu.roll(x, half, 1)
    return jnp.where(first_half, -fwd, bwd)


def _swa_kernel(q_ref, kp_ref, kc_ref, vp_ref, vc_ref, cp_ref, sp_ref, cc_ref,
                sc_ref, sink_ref, o_ref, *, kv_heads):
    w = SWA_BLOCK
    d = SWA_HEAD_DIM
    blk = pl.program_id(1)
    lane = lax.broadcasted_iota(I32, (w, LANES), 1)
    first_half = (lane % d) < (d // 2)
    low_head = lane < d

    def rope(x, cos, sin, fh):
        return x * cos + _rotate_half(x, fh) * sin

    cos_c, sin_c = cc_ref[...], sc_ref[...]
    cos_p, sin_p = cp_ref[...], sp_ref[...]

    kw = kv_heads * d
    lane_k = lax.broadcasted_iota(I32, (w, kw), 1)
    fh_k = (lane_k % d) < (d // 2)
    reps = kw // LANES
    tile = lambda t: jnp.concatenate([t] * reps, axis=1) if reps > 1 else t
    k_prev = rope(kp_ref[0].astype(F32), tile(cos_p), tile(sin_p), fh_k)
    k_cur = rope(kc_ref[0].astype(F32), tile(cos_c), tile(sin_c), fh_k)
    k_all = jnp.concatenate([k_prev, k_cur], axis=0)
    v_all = jnp.concatenate([vp_ref[0], vc_ref[0]], axis=0)

    pairs_per_kv = SWA_GROUP // 2
    m_rows = pairs_per_kv * w
    qr = lax.broadcasted_iota(I32, (m_rows, w), 0) % w
    kc = lax.broadcasted_iota(I32, (m_rows, w), 1)
    from_prev = kc > qr
    no_prev = from_prev & (blk == 0)
    lane_v = lax.broadcasted_iota(I32, (2 * w, LANES), 1)

    scale = d ** -0.5
    chains = []
    for g in range(kv_heads):
        kg = k_all[:, g * d:(g + 1) * d]
        vg = v_all[:, g * d:(g + 1) * d]
        zk = jnp.zeros_like(kg)
        zv = jnp.zeros_like(vg)
        k_lo = jnp.concatenate([kg, zk], axis=1).astype(BF16)
        k_hi = jnp.concatenate([zk, kg], axis=1).astype(BF16)
        v_lo = jnp.where(lane_v == d, 1.0, jnp.concatenate([vg, zv], axis=1))
        v_hi = jnp.where(lane_v == 0, 1.0, jnp.concatenate([zv, vg], axis=1))
        q_parts, sink_lo, sink_hi = [], [], []
        for p in range(pairs_per_kv):
            cb = g * pairs_per_kv + p
            q2 = q_ref[0, :, cb * LANES:(cb + 1) * LANES].astype(F32)
            q_parts.append((rope(q2, cos_c, sin_c, first_half) * scale).astype(BF16))
            sink2 = sink_ref[:, cb * LANES:(cb + 1) * LANES]
            for sel, dst in ((low_head, sink_lo), (~low_head, sink_hi)):
                s1 = jnp.max(jnp.where(sel[:1], sink2, NEG_BIG), axis=-1, keepdims=True)
                dst.append(jnp.broadcast_to(s1, (w, 1)))
        q_st = jnp.concatenate(q_parts, axis=0)
        chains.append((q_st, k_lo, v_lo, jnp.concatenate(sink_lo, axis=0), d))
        chains.append((q_st, k_hi, v_hi, jnp.concatenate(sink_hi, axis=0), 0))

    scores = [lax.dot_general(q_st, kk, (((1,), (1,)), ((), ())), preferred_element_type=F32)
              for q_st, kk, _, _, _ in chains]
    probs, row_max = [], []
    for s, (_, _, _, sink, _) in zip(scores, chains):
        s = jnp.where(from_prev, s[:, :w], s[:, w:])
        s = jnp.where(no_prev, NEG_BIG, s)
        m = jnp.maximum(jnp.max(s, axis=-1, keepdims=True), sink)
        p_un = jnp.exp(s - m)
        probs.append(jnp.concatenate([jnp.where(from_prev, p_un, 0.0),
                                      jnp.where(from_prev, 0.0, p_un)], axis=1).astype(BF16))
        row_max.append(m)
    pv = [jnp.dot(p2, vv, preferred_element_type=F32)
          for p2, (_, _, vv, _, _) in zip(probs, chains)]
    outs = []
    for o, m, (_, _, _, sink, sum_lane) in zip(pv, row_max, chains):
        denom = o[:, sum_lane:sum_lane + 1] + jnp.exp(sink - m)
        outs.append(o / denom)
    low_rows = lax.broadcasted_iota(I32, (m_rows, LANES), 1) < d
    for g in range(kv_heads):
        acc = jnp.where(low_rows, outs[2 * g], outs[2 * g + 1])
        for p in range(pairs_per_kv):
            cb = g * pairs_per_kv + p
            o_ref[0, :, cb * LANES:(cb + 1) * LANES] = acc[p * w:(p + 1) * w].astype(o_ref.dtype)


def _swa(proj, cos, sin, sinks_lane, *, q_heads, kv_heads, col_q, col_k, col_v):
    bsz, t, _ = proj.shape
    w = SWA_BLOCK
    qw = q_heads * SWA_HEAD_DIM
    kw = kv_heads * SWA_HEAD_DIM
    prev = lambda i: jnp.maximum(i - 1, 0)
    kern = functools.partial(_swa_kernel, kv_heads=kv_heads)
    tab_p = pl.BlockSpec((w, LANES), lambda b, i: (b * (t // w) + prev(i), 0))
    tab_c = pl.BlockSpec((w, LANES), lambda b, i: (b * (t // w) + i, 0))
    return pl.pallas_call(
        kern,
        grid=(bsz, t // w),
        in_specs=[
            pl.BlockSpec((1, w, qw), lambda b, i: (b, i, col_q // qw)),
            pl.BlockSpec((1, w, kw), lambda b, i: (b, prev(i), col_k // kw)),
            pl.BlockSpec((1, w, kw), lambda b, i: (b, i, col_k // kw)),
            pl.BlockSpec((1, w, kw), lambda b, i: (b, prev(i), col_v // kw)),
            pl.BlockSpec((1, w, kw), lambda b, i: (b, i, col_v // kw)),
            tab_p, tab_p, tab_c, tab_c,
            pl.BlockSpec((1, qw), lambda b, i: (0, 0)),
        ],
        out_specs=pl.BlockSpec((1, w, qw), lambda b, i: (b, i, 0)),
        out_shape=jax.ShapeDtypeStruct((bsz, t, qw), BF16),
        compiler_params=_cparams(("parallel", "arbitrary")),
        name="swa",
    )(proj, proj, proj, proj, proj, cos, sin, cos, sin, sinks_lane)


def _router_kernel(h_ref, g_ref, wr_ref, br_ref, u_ref, rt_ref):
    h = h_ref[...]
    ms = jnp.mean(h * h, axis=-1, keepdims=True)
    u = h * lax.rsqrt(ms + RMS_EPS) * g_ref[...]
    u_ref[...] = u
    u_hi = u.astype(BF16)
    u_lo = (u - u_hi.astype(F32)).astype(BF16)
    both = jnp.dot(u_hi, wr_ref[...], preferred_element_type=F32)
    logits = (both[:, :LANES] + both[:, LANES:]
              + jnp.dot(u_lo, wr_ref[:, :LANES], preferred_element_type=F32) + br_ref[...])
    lane = lax.broadcasted_iota(I32, logits.shape, 1)
    is_group = lane < N_GROUPS
    gl = jnp.where(is_group, logits, NEG_BIG)
    gmax = jnp.max(gl, axis=-1, keepdims=True)
    gidx = jnp.min(jnp.where(gl == gmax, lane, LANES), axis=-1, keepdims=True)
    gsum = jnp.sum(jnp.where(is_group, jnp.exp(gl - gmax), 0.0), axis=-1, keepdims=True)
    g_w = 1.0 / gsum
    lo = N_GROUPS + gidx * EXPERTS_PER_GROUP
    sel = (lane >= lo) & (lane < lo + EXPERTS_PER_GROUP)
    el = jnp.where(sel, logits, NEG_BIG)
    m1 = jnp.max(el, axis=-1, keepdims=True)
    i1 = jnp.min(jnp.where(el == m1, lane, LANES), axis=-1, keepdims=True)
    el2 = jnp.where(lane == i1, NEG_BIG, el)
    m2 = jnp.max(el2, axis=-1, keepdims=True)
    i2 = jnp.min(jnp.where(el2 == m2, lane, LANES), axis=-1, keepdims=True)
    t = jnp.exp(m2 - m1)
    w1 = g_w / (1.0 + t)
    w2 = g_w * t / (1.0 + t)
    e1 = (i1 - N_GROUPS).astype(F32)
    e2 = (i2 - N_GROUPS).astype(F32)
    rt_ref[...] = jnp.where(lane == 0, e1,
                  jnp.where(lane == 1, e2,
                  jnp.where(lane == 2, w1,
                  jnp.where(lane == 3, w2, 0.0))))


def _router(h2d, g, wr, br, *, rows):
    n, d = h2d.shape
    return pl.pallas_call(
        _router_kernel,
        grid=(n // rows,),
        in_specs=[pl.BlockSpec((rows, d), lambda i: (i, 0)),
                  pl.BlockSpec((1, d), lambda i: (0, 0)),
                  pl.BlockSpec((d, 2 * LANES), lambda i: (0, 0)),
                  pl.BlockSpec((1, LANES), lambda i: (0, 0))],
        out_specs=[pl.BlockSpec((rows, d), lambda i: (i, 0)),
                   pl.BlockSpec((rows, LANES), lambda i: (i, 0))],
        out_shape=[jax.ShapeDtypeStruct((n, d), F32),
                   jax.ShapeDtypeStruct((n, LANES), F32)],
        compiler_params=_cparams(("parallel",)),
        name="router",
    )(h2d, g, wr, br)


def _expert_kernel(te_ref, nu_ref, tv_ref, src_ref, nxt_ref, dst_ref, u_ref, wg_ref, wu_ref,
                   wd_ref, y_ref, xbuf, ybuf, xb_ref, gsem, ssem, *, tm):
    i = pl.program_id(0)
    part = pl.program_id(1)
    n_used = nu_ref[0]

    def gather(idx_ref, slot, n_valid):
        def issue(k, carry):
            t = idx_ref[0, 0, k]
            pltpu.make_async_copy(u_ref.at[pl.ds(t, 1), :],
                                  xbuf.at[slot, pl.ds(k, 1), :], gsem.at[slot]).start()
            return carry
        lax.fori_loop(0, n_valid, issue, 0)

    def scatter(slot, n_valid):
        def issue(k, carry):
            r = dst_ref[0, 0, k]
            pltpu.make_async_copy(ybuf.at[slot, pl.ds(k, 1), :],
                                  y_ref.at[pl.ds(r, 1), :], ssem.at[slot]).start()
            return carry
        lax.fori_loop(0, n_valid, issue, 0)

    def rows_wait(copy_of, n_valid):
        bit = tm
        while bit >= 1:
            @pl.when((n_valid & bit) != 0)
            def _(bit=bit):
                copy_of(bit).wait()
            bit //= 2

    def gather_wait(slot, n_valid):
        rows_wait(lambda n: pltpu.make_async_copy(
            u_ref.at[pl.ds(0, n), :], xbuf.at[slot, pl.ds(0, n), :], gsem.at[slot]), n_valid)

    def scatter_wait(slot, n_valid):
        rows_wait(lambda n: pltpu.make_async_copy(
            ybuf.at[slot, pl.ds(0, n), :], y_ref.at[pl.ds(0, n), :], ssem.at[slot]), n_valid)

    @pl.when(i < n_used)
    def _():
        slot = i % 2

        @pl.when(part == 0)
        def _():
            @pl.when(i == 0)
            def _():
                xbuf[...] = jnp.zeros_like(xbuf)
                gather(src_ref, 0, tv_ref[0])

            @pl.when(i + 1 < n_used)
            def _():
                gather(nxt_ref, 1 - slot, tv_ref[i + 1])

            gather_wait(slot, tv_ref[i])

            @pl.when(i >= 2)
            def _():
                scatter_wait(slot, tv_ref[i - 2])

            xb_ref[...] = xbuf[slot].astype(BF16)

        x = xb_ref[...]
        a = jnp.dot(x, wg_ref[0].astype(BF16), preferred_element_type=F32)
        b = jnp.dot(x, wu_ref[0].astype(BF16), preferred_element_type=F32)
        hid = (a / (1.0 + jnp.exp(-a)) * b).astype(BF16)
        y = jnp.dot(hid, wd_ref[0].astype(BF16), preferred_element_type=F32)

        @pl.when(part == 0)
        def _():
            ybuf[slot] = y

        @pl.when(part == 1)
        def _():
            ybuf[slot] = ybuf[slot] + y
            scatter(slot, tv_ref[i])

            @pl.when(i == n_used - 1)
            def _():
                scatter_wait(slot, tv_ref[i])

                @pl.when(i >= 1)
                def _():
                    scatter_wait(1 - slot, tv_ref[i - 1])


def _experts(tile_expert, n_used, tile_valid, src_tok, out_row, u2, wg, wu, wd, n_out_rows,
             *, tm):
    n_tiles = src_tok.shape[0]
    _, d = u2.shape
    _, _, f = wg.shape
    kern = functools.partial(_expert_kernel, tm=tm)
    idx_spec = lambda off: pl.BlockSpec(
        (1, 1, tm), lambda i, p, te, nu, tv: (jnp.minimum(i + off, n_tiles - 1), 0, 0),
        memory_space=pltpu.SMEM)
    def half(i, p, nu):
        last = nu[0] - 1
        return jnp.where(i <= last, (i + p) % 2, (last + 1) % 2)
    fh = f // 2
    grid_spec = pltpu.PrefetchScalarGridSpec(
        num_scalar_prefetch=3,
        grid=(n_tiles, 2),
        in_specs=[idx_spec(0), idx_spec(1), idx_spec(0),
                  pl.BlockSpec(memory_space=pl.ANY),
                  pl.BlockSpec((1, d, fh), lambda i, p, te, nu, tv: (te[i], 0, half(i, p, nu))),
                  pl.BlockSpec((1, d, fh), lambda i, p, te, nu, tv: (te[i], 0, half(i, p, nu))),
                  pl.BlockSpec((1, fh, d), lambda i, p, te, nu, tv: (te[i], half(i, p, nu), 0))],
        out_specs=pl.BlockSpec(memory_space=pl.ANY),
        scratch_shapes=[pltpu.VMEM((2, tm, d), F32), pltpu.VMEM((2, tm, d), F32),
                        pltpu.VMEM((tm, d), BF16),
                        pltpu.SemaphoreType.DMA((2,)), pltpu.SemaphoreType.DMA((2,))],
    )
    return pl.pallas_call(
        kern,
        grid_spec=grid_spec,
        out_shape=jax.ShapeDtypeStruct((n_out_rows, d), F32),
        compiler_params=_cparams(("arbitrary", "arbitrary")),
        name="experts",
    )(tile_expert, n_used, tile_valid, src_tok, src_tok, out_row, u2, wg, wu, wd)


def _combine_kernel(h_ref, y0_ref, y1_ref, rt_ref, g_ref, o_ref, *, final_norm):
    rt = rt_ref[...]
    w0 = rt[:, 2:3]
    w1 = rt[:, 3:4]
    h = h_ref[...] + w0 * y0_ref[...] + w1 * y1_ref[...]
    if final_norm:
        ms = jnp.mean(h * h, axis=-1, keepdims=True)
        h = h * lax.rsqrt(ms + RMS_EPS) * g_ref[...]
    o_ref[...] = h


def _combine(h2d, y, route, g, *, rows, final_norm):
    n, d = h2d.shape
    nb = n // rows
    return pl.pallas_call(
        functools.partial(_combine_kernel, final_norm=final_norm),
        grid=(nb,),
        in_specs=[pl.BlockSpec((rows, d), lambda i: (i, 0)),
                  pl.BlockSpec((rows, d), lambda i: (i, 0)),
                  pl.BlockSpec((rows, d), lambda i: (nb + i, 0)),
                  pl.BlockSpec((rows, LANES), lambda i: (i, 0)),
                  pl.BlockSpec((1, d), lambda i: (0, 0))],
        out_specs=pl.BlockSpec((rows, d), lambda i: (i, 0)),
        out_shape=jax.ShapeDtypeStruct((n, d), F32),
        compiler_params=_cparams(("parallel",)),
        name="combine",
    )(h2d, y, y, route, g)


def _dispatch_plan(eid, n_experts, tm, p_rows):
    n = eid.shape[0]
    n_flat = TOP_K * n
    n_tiles = p_rows // tm
    e_flat = eid.T.reshape(-1)
    onehot = (e_flat[:, None] == jnp.arange(n_experts, dtype=I32)[None, :]).astype(I32)
    csum = jnp.cumsum(onehot, axis=0)
    counts = csum[-1]
    pcounts = ((counts + tm - 1) // tm) * tm
    pend = jnp.cumsum(pcounts)
    pstart = pend - pcounts
    dest = jnp.sum(onehot * (csum - 1 + pstart[None, :]), axis=1)
    n_used = (pend[-1] // tm).astype(I32)
    inv = jnp.full((p_rows,), n_flat, I32).at[dest].set(
        jnp.arange(n_flat, dtype=I32), unique_indices=True)
    is_pad = inv >= n_flat
    src_tok = jnp.where(is_pad, 0, inv % n)
    out_row = jnp.where(is_pad, 0, inv)
    tile_start = jnp.arange(n_tiles, dtype=I32) * tm
    used_start = jnp.minimum(tile_start, pend[-1] - tm)
    tile_onehot = ((used_start[:, None] >= pstart[None, :])
                   & (used_start[:, None] < pend[None, :])).astype(I32)
    tile_expert = jnp.sum(tile_onehot * jnp.arange(n_experts, dtype=I32)[None, :], axis=1)
    valid_end = jnp.sum(tile_onehot * (pstart + counts)[None, :], axis=1)
    tile_valid = jnp.where(tile_start < pend[-1], jnp.clip(valid_end - tile_start, 0, tm), 0)
    return (src_tok.reshape(n_tiles, 1, tm), out_row.reshape(n_tiles, 1, tm),
            tile_expert, n_used.reshape(1), tile_valid.astype(I32))


def _tiles(n_tokens, d_model):
    return dict(
        norm_rows=256,
        wide_rows=512,
        in_tm=2048, in_tn=512,
        out_tm=1024, out_tn=1024,
        gla_tc=512,
        moe_tm=256,
    )


def kernel(x, positions, norm_mix_g, w_in, gla_gate_w2, gla_gate_b, gla_out_norm_g,
           swa_sinks, w_out, norm_ffn_g, router_group_w, router_group_b,
           router_expert_w, router_expert_b, expert_w_gate, expert_w_up,
           expert_w_down, norm_final_g):
    bsz, t, d = x.shape
    n = bsz * t
    depth = w_in.shape[0]
    cfg = _tiles(n, d)

    rank = gla_gate_w2.shape[1]
    key_w = gla_gate_w2.shape[2]
    val_w = GLA_HEADS * gla_out_norm_g.shape[1]
    dk, dv = key_w // GLA_HEADS, val_w // GLA_HEADS
    q_heads = swa_sinks.shape[1]
    kv_heads = q_heads // SWA_GROUP
    sq_w, skv_w = q_heads * SWA_HEAD_DIM, kv_heads * SWA_HEAD_DIM
    n_experts = expert_w_gate.shape[1]
    tm = cfg["moe_tm"]
    p_rows = TOP_K * n + n_experts * tm

    c_gq, c_gk, c_gv, c_gr = 0, key_w, 2 * key_w, 2 * key_w + val_w
    c_ga = 2 * key_w + 2 * val_w
    c_sq = c_ga + rank
    m_sq = c_ga
    m_sk, m_sv = m_sq + sq_w, m_sq + sq_w + skv_w

    half = SWA_HEAD_DIM // 2
    inv_freq = ROPE_THETA ** (-jnp.arange(half, dtype=F32) / half)
    freq_lane = jnp.tile(inv_freq, LANES // half).reshape(1, LANES)
    posf = jnp.broadcast_to(positions.astype(F32).reshape(n, 1), (n, LANES))
    cos_t, sin_t = _rope_tables(posf, freq_lane, rows=1024)

    h2d = x.reshape(n, d)
    for l in range(depth):
        w_bf = w_in[l].astype(BF16)
        w_swa = w_bf[:, c_sq:]
        w_ga = jnp.pad(w_bf[:, c_ga:c_sq], ((0, 0), (0, LANES - rank)))
        w2p = jnp.pad(gla_gate_w2[l], ((0, LANES - rank), (0, 0))).astype(BF16)

        u, ga = _norm_gate(h2d, norm_mix_g[l].reshape(1, d), w_ga, rows=cfg["wide_rows"])
        proj = _in_proj(u, w_bf, w_swa, c_ga, tm=cfg["in_tm"], tn=cfg["in_tn"])
        proj3 = proj.reshape(bsz, t, -1)

        o_gla = _gla(proj3, ga.reshape(bsz, t, LANES), w2p,
                     gla_gate_b[l].reshape(1, key_w), gla_out_norm_g[l].reshape(1, dv),
                     heads=GLA_HEADS, dk=dk, dv=dv, tc=cfg["gla_tc"],
                     col_q=c_gq, col_k=c_gk, col_v=c_gv, col_r=c_gr)
        sinks_lane = jnp.repeat(swa_sinks[l], SWA_HEAD_DIM).reshape(1, sq_w)
        o_swa = _swa(proj3, cos_t, sin_t, sinks_lane, q_heads=q_heads, kv_heads=kv_heads,
                     col_q=m_sq, col_k=m_sk, col_v=m_sv)

        h2d = _out_proj(o_gla.reshape(n, val_w), o_swa.reshape(n, sq_w),
                        w_out[l].astype(BF16), h2d, tm=cfg["out_tm"], tn=cfg["out_tn"])

        wr = jnp.concatenate(
            [router_group_w[l],
             jnp.transpose(router_expert_w[l], (1, 0, 2)).reshape(d, n_experts)], axis=1)
        n_router = wr.shape[1]
        wr = jnp.pad(wr, ((0, 0), (0, LANES - n_router)))
        wr_hi = wr.astype(BF16)
        wr = jnp.concatenate([wr_hi, (wr - wr_hi.astype(F32)).astype(BF16)], axis=1)
        br = jnp.pad(jnp.concatenate([router_group_b[l], router_expert_b[l].reshape(-1)]),
                     (0, LANES - n_router)).reshape(1, LANES)
        u2, route = _router(h2d, norm_ffn_g[l].reshape(1, d), wr, br, rows=cfg["wide_rows"])

        eid = route[:, :TOP_K].astype(I32)
        src_tok, out_row, tile_expert, n_used, tile_valid = _dispatch_plan(
            eid, n_experts, tm, p_rows)
        y_tok = _experts(tile_expert, n_used, tile_valid, src_tok, out_row, u2,
                         expert_w_gate[l], expert_w_up[l], expert_w_down[l],
                         TOP_K * n, tm=tm)

        h2d = _combine(h2d, y_tok, route, norm_final_g.reshape(1, d),
                       rows=cfg["norm_rows"], final_norm=(l == depth - 1))
    return h2d.reshape(bsz, t, d)
```

```python
import functools

import jax
import jax.numpy as jnp
from jax import lax
from jax.experimental import pallas as pl
from jax.experimental.pallas import tpu as pltpu

F32 = jnp.float32
BF16 = jnp.bfloat16
I32 = jnp.int32

RMS_EPS = 1e-6

GLA_HEADS = 8
GLA_GATE_TAU = 16.0
GLA_CHUNK = 64
GLA_HEADS_PER_STEP = 8
SWA_HEAD_DIM = 64
SWA_GROUP = 8
SWA_BLOCK = 128
ROPE_THETA = 10000.0
N_GROUPS = 8
EXPERTS_PER_GROUP = 8
TOP_K = 2
ROW_DMA_UNROLL = 16

LANES = 128
VMEM_LIMIT_BYTES = 56 * 1024 * 1024

NEG_BIG = -1e30


def _cparams(semantics):
    return pltpu.CompilerParams(dimension_semantics=semantics,
                                vmem_limit_bytes=VMEM_LIMIT_BYTES)


def _norm_gate_kernel(x_ref, g_ref, wga_ref, u_ref, ga_ref):
    x = x_ref[...]
    ms = jnp.mean(x * x, axis=-1, keepdims=True)
    u = (x * lax.rsqrt(ms + RMS_EPS) * g_ref[...]).astype(BF16)
    u_ref[...] = u
    ga_ref[...] = jnp.dot(u, wga_ref[...], preferred_element_type=F32)


def _norm_gate(x2d, g, wga, *, rows):
    n, d = x2d.shape
    return pl.pallas_call(
        _norm_gate_kernel,
        grid=(n // rows,),
        in_specs=[pl.BlockSpec((rows, d), lambda i: (i, 0)),
                  pl.BlockSpec((1, d), lambda i: (0, 0)),
                  pl.BlockSpec((d, LANES), lambda i: (0, 0))],
        out_specs=[pl.BlockSpec((rows, d), lambda i: (i, 0)),
                   pl.BlockSpec((rows, LANES), lambda i: (i, 0))],
        out_shape=[jax.ShapeDtypeStruct((n, d), BF16),
                   jax.ShapeDtypeStruct((n, LANES), F32)],
        compiler_params=_cparams(("parallel",)),
        name="norm_gate",
    )(x2d, g, wga)


def _in_proj_kernel(a_ref, wa_ref, wb_ref, o_ref, *, n_a):
    j = pl.program_id(1)

    @pl.when(j < n_a)
    def _():
        o_ref[...] = jnp.dot(a_ref[...], wa_ref[...],
                             preferred_element_type=F32).astype(o_ref.dtype)

    @pl.when(j >= n_a)
    def _():
        o_ref[...] = jnp.dot(a_ref[...], wb_ref[...],
                             preferred_element_type=F32).astype(o_ref.dtype)


def _in_proj(a, wa, wb, n_a_cols, *, tm, tn):
    m, k = a.shape
    n_a = n_a_cols // tn
    n_b = wb.shape[1] // tn
    return pl.pallas_call(
        functools.partial(_in_proj_kernel, n_a=n_a),
        grid=(m // tm, n_a + n_b),
        in_specs=[pl.BlockSpec((tm, k), lambda i, j: (i, 0)),
                  pl.BlockSpec((k, tn), lambda i, j: (0, jnp.minimum(j, n_a - 1))),
                  pl.BlockSpec((k, tn), lambda i, j: (0, jnp.maximum(j - n_a, 0)))],
        out_specs=pl.BlockSpec((tm, tn), lambda i, j: (i, j)),
        out_shape=jax.ShapeDtypeStruct((m, n_a_cols + wb.shape[1]), BF16),
        compiler_params=_cparams(("parallel", "arbitrary")),
        name="in_proj",
    )(a, wa, wb)


def _out_proj_kernel(a_ref, b_ref, wa_ref, wb_ref, x_ref, o_ref):
    acc = jnp.dot(a_ref[...], wa_ref[...], preferred_element_type=F32)
    acc = acc + jnp.dot(b_ref[...], wb_ref[...], preferred_element_type=F32)
    o_ref[...] = x_ref[...] + acc


def _out_proj(a, b, w, x2d, *, tm, tn):
    m, ka = a.shape
    _, kb = b.shape
    _, n = w.shape
    return pl.pallas_call(
        _out_proj_kernel,
        grid=(m // tm, n // tn),
        in_specs=[pl.BlockSpec((tm, ka), lambda i, j: (i, 0)),
                  pl.BlockSpec((tm, kb), lambda i, j: (i, 0)),
                  pl.BlockSpec((ka, tn), lambda i, j: (0, j)),
                  pl.BlockSpec((kb, tn), lambda i, j: (ka // kb, j)),
                  pl.BlockSpec((tm, tn), lambda i, j: (i, j))],
        out_specs=pl.BlockSpec((tm, tn), lambda i, j: (i, j)),
        out_shape=jax.ShapeDtypeStruct((m, n), F32),
        compiler_params=_cparams(("parallel", "arbitrary")),
        name="out_proj",
    )(a, b, w, w, x2d)


def _gla_kernel(q_ref, k_ref, v_ref, r_ref, ga_ref, w2_ref, gb_ref, gn_ref, tri_ref,
                o_ref, st_ref, *, chunk, dk, dv):
    @pl.when(pl.program_id(2) == 0)
    def _():
        st_ref[...] = jnp.zeros_like(st_ref)

    tc = q_ref.shape[1]
    n_chunks = tc // chunk
    heads = range(q_ref.shape[2] // dk)
    rows = lambda a, c: a[c * chunk:(c + 1) * chunk, :]
    last = lambda a, c: a[(c + 1) * chunk - 1:(c + 1) * chunk, :]
    contract_last = (((1,), (1,)), ((), ()))
    contract_first = (((0,), (0,)), ((), ()))
    key = lambda h: slice(h * dk, (h + 1) * dk)
    val = lambda h: slice(h * dv, (h + 1) * dv)

    tri = tri_ref[...]
    tri_b = tri.astype(BF16)
    ga = ga_ref[0].astype(BF16)
    k, v, b = [], [], []
    for h in heads:
        k.append(k_ref[0, :, key(h)].astype(F32))
        v.append(v_ref[0, :, val(h)])
        z = jnp.dot(ga, w2_ref[:, key(h)], preferred_element_type=F32) + gb_ref[:, key(h)]
        g = -(jnp.maximum(-z, 0.0) + jnp.log1p(jnp.exp(-jnp.abs(z)))) / GLA_GATE_TAU
        g_hi = g.astype(BF16)
        g_lo = (g - g_hi.astype(F32)).astype(BF16)
        b.append(jnp.dot(tri_b, g_hi, preferred_element_type=F32)
                 + jnp.dot(tri_b, g_lo, preferred_element_type=F32))
    q_in, o = [], []
    for h in heads:
        q = q_ref[0, :, key(h)].astype(F32) * (dk ** -0.5)
        q_in.append((q * jnp.exp(b[h])).astype(BF16))
        k_in = (k[h] * jnp.exp(-b[h])).astype(BF16)
        a = lax.dot_general(q_in[h], k_in, contract_last, preferred_element_type=F32)
        a = jnp.where(tri > 0.0, a, 0.0).astype(BF16)
        o.append(jnp.dot(a, v[h], preferred_element_type=F32))

    kv_t = [[lax.dot_general(
        rows(v[h], c),
        (rows(k[h], c) * jnp.exp(last(b[h], c) - rows(b[h], c))).astype(BF16),
        contract_first, preferred_element_type=F32) for c in range(n_chunks)] for h in heads]
    st_in = []
    for h in heads:
        st = st_ref[h]
        st_in.append([])
        for c in range(n_chunks):
            st_in[h].append(st.astype(BF16))
            st = st * jnp.exp(last(b[h], c)) + kv_t[h][c]
        st_ref[h] = st
    for h in heads:
        o_inter = [lax.dot_general(rows(q_in[h], c), st_in[h][c], contract_last,
                                   preferred_element_type=F32) for c in range(n_chunks)]
        oh = o[h] + jnp.concatenate(o_inter, axis=0)
        ms = jnp.mean(oh * oh, axis=-1, keepdims=True)
        oh = oh * lax.rsqrt(ms + RMS_EPS) * gn_ref[...]
        r = r_ref[0, :, val(h)].astype(F32)
        oh = oh * (r / (1.0 + jnp.exp(-r)))
        o_ref[0, :, val(h)] = oh.astype(o_ref.dtype)


def _gla(proj, ga, w2p, gb, gn, *, heads, dk, dv, tc, col_q, col_k, col_v, col_r):
    bsz, t, _ = proj.shape
    kern = functools.partial(_gla_kernel, chunk=GLA_CHUNK, dk=dk, dv=dv)
    idx = jnp.arange(tc, dtype=I32)
    tri = ((idx[:, None] // GLA_CHUNK == idx[None, :] // GLA_CHUNK)
           & (idx[None, :] <= idx[:, None])).astype(F32)
    hp = GLA_HEADS_PER_STEP
    kw, vw = hp * dk, hp * dv
    return pl.pallas_call(
        kern,
        grid=(bsz, heads // hp, t // tc),
        in_specs=[
            pl.BlockSpec((1, tc, kw), lambda b, h, i: (b, i, col_q // kw + h)),
            pl.BlockSpec((1, tc, kw), lambda b, h, i: (b, i, col_k // kw + h)),
            pl.BlockSpec((1, tc, vw), lambda b, h, i: (b, i, col_v // vw + h)),
            pl.BlockSpec((1, tc, vw), lambda b, h, i: (b, i, col_r // vw + h)),
            pl.BlockSpec((1, tc, LANES), lambda b, h, i: (b, i, 0)),
            pl.BlockSpec((LANES, kw), lambda b, h, i: (0, h)),
            pl.BlockSpec((1, kw), lambda b, h, i: (0, h)),
            pl.BlockSpec((1, dv), lambda b, h, i: (0, 0)),
            pl.BlockSpec((tc, tc), lambda b, h, i: (0, 0)),
        ],
        out_specs=pl.BlockSpec((1, tc, vw), lambda b, h, i: (b, i, h)),
        out_shape=jax.ShapeDtypeStruct((bsz, t, heads * dv), BF16),
        scratch_shapes=[pltpu.VMEM((hp, dv, dk), F32)],
        compiler_params=_cparams(("parallel", "parallel", "arbitrary")),
        name="gla",
    )(proj, proj, proj, proj, ga, w2p, gb, gn, tri)


def _rope_table_kernel(pos_ref, freq_ref, cos_ref, sin_ref):
    ang = pos_ref[...] * freq_ref[...]
    cos_ref[...] = jnp.cos(ang)
    sin_ref[...] = jnp.sin(ang)


def _rope_tables(posf, freq, *, rows):
    n = posf.shape[0]
    spec = pl.BlockSpec((rows, LANES), lambda i: (i, 0))
    return pl.pallas_call(
        _rope_table_kernel,
        grid=(n // rows,),
        in_specs=[spec, pl.BlockSpec((1, LANES), lambda i: (0, 0))],
        out_specs=[spec, spec],
        out_shape=[jax.ShapeDtypeStruct((n, LANES), F32)] * 2,
        compiler_params=_cparams(("parallel",)),
        name="rope_tables",
    )(posf, freq)


def _rotate_half(x, first_half):
    half = SWA_HEAD_DIM // 2
    width = x.shape[-1]
    fwd = pltpu.roll(x, width - half, 1)
    bwd = pltpu.roll(x, half, 1)
    return jnp.where(first_half, -fwd, bwd)


def _swa_kernel(q_ref, kp_ref, kc_ref, vp_ref, vc_ref, cp_ref, sp_ref, cc_ref,
                sc_ref, sink_ref, o_ref, *, kv_heads):
    w = SWA_BLOCK
    d = SWA_HEAD_DIM
    blk = pl.program_id(1)
    lane = lax.broadcasted_iota(I32, (w, LANES), 1)
    first_half = (lane % d) < (d // 2)
    low_head = lane < d

    def rope(x, cos, sin, fh):
        return x * cos + _rotate_half(x, fh) * sin

    cos_c, sin_c = cc_ref[...], sc_ref[...]
    cos_p, sin_p = cp_ref[...], sp_ref[...]

    kw = kv_heads * d
    lane_k = lax.broadcasted_iota(I32, (w, kw), 1)
    fh_k = (lane_k % d) < (d // 2)
    reps = kw // LANES
    tile = lambda t: jnp.concatenate([t] * reps, axis=1) if reps > 1 else t
    k_prev = rope(kp_ref[0].astype(F32), tile(cos_p), tile(sin_p), fh_k)
    k_cur = rope(kc_ref[0].astype(F32), tile(cos_c), tile(sin_c), fh_k)
    k_all = jnp.concatenate([k_prev, k_cur], axis=0)
    v_all = jnp.concatenate([vp_ref[0], vc_ref[0]], axis=0)

    pairs_per_kv = SWA_GROUP // 2
    m_rows = pairs_per_kv * w
    qr = lax.broadcasted_iota(I32, (m_rows, w), 0) % w
    kc = lax.broadcasted_iota(I32, (m_rows, w), 1)
    from_prev = kc > qr
    no_prev = from_prev & (blk == 0)
    lane_v = lax.broadcasted_iota(I32, (2 * w, LANES), 1)

    scale = d ** -0.5
    chains = []
    for g in range(kv_heads):
        kg = k_all[:, g * d:(g + 1) * d]
        vg = v_all[:, g * d:(g + 1) * d]
        zk = jnp.zeros_like(kg)
        zv = jnp.zeros_like(vg)
        k_lo = jnp.concatenate([kg, zk], axis=1).astype(BF16)
        k_hi = jnp.concatenate([zk, kg], axis=1).astype(BF16)
        v_lo = jnp.where(lane_v == d, 1.0, jnp.concatenate([vg, zv], axis=1))
        v_hi = jnp.where(lane_v == 0, 1.0, jnp.concatenate([zv, vg], axis=1))
        q_parts, sink_lo, sink_hi = [], [], []
        for p in range(pairs_per_kv):
            cb = g * pairs_per_kv + p
            q2 = q_ref[0, :, cb * LANES:(cb + 1) * LANES].astype(F32)
            q_parts.append((rope(q2, cos_c, sin_c, first_half) * scale).astype(BF16))
            sink2 = sink_ref[:, cb * LANES:(cb + 1) * LANES]
            for sel, dst in ((low_head, sink_lo), (~low_head, sink_hi)):
                s1 = jnp.max(jnp.where(sel[:1], sink2, NEG_BIG), axis=-1, keepdims=True)
                dst.append(jnp.broadcast_to(s1, (w, 1)))
        q_st = jnp.concatenate(q_parts, axis=0)
        chains.append((q_st, k_lo, v_lo, jnp.concatenate(sink_lo, axis=0), d))
        chains.append((q_st, k_hi, v_hi, jnp.concatenate(sink_hi, axis=0), 0))

    scores = [lax.dot_general(q_st, kk, (((1,), (1,)), ((), ())), preferred_element_type=F32)
              for q_st, kk, _, _, _ in chains]
    probs, row_max = [], []
    for s, (_, _, _, sink, _) in zip(scores, chains):
        s = jnp.where(from_prev, s[:, :w], s[:, w:])
        s = jnp.where(no_prev, NEG_BIG, s)
        m = jnp.maximum(jnp.max(s, axis=-1, keepdims=True), sink)
        p_un = jnp.exp(s - m)
        probs.append(jnp.concatenate([jnp.where(from_prev, p_un, 0.0),
                                      jnp.where(from_prev, 0.0, p_un)], axis=1).astype(BF16))
        row_max.append(m)
    pv = [jnp.dot(p2, vv, preferred_element_type=F32)
          for p2, (_, _, vv, _, _) in zip(probs, chains)]
    outs = []
    for o, m, (_, _, _, sink, sum_lane) in zip(pv, row_max, chains):
        denom = o[:, sum_lane:sum_lane + 1] + jnp.exp(sink - m)
        outs.append(o / denom)
    low_rows = lax.broadcasted_iota(I32, (m_rows, LANES), 1) < d
    for g in range(kv_heads):
        acc = jnp.where(low_rows, outs[2 * g], outs[2 * g + 1])
        for p in range(pairs_per_kv):
            cb = g * pairs_per_kv + p
            o_ref[0, :, cb * LANES:(cb + 1) * LANES] = acc[p * w:(p + 1) * w].astype(o_ref.dtype)


def _swa(proj, cos, sin, sinks_lane, *, q_heads, kv_heads, col_q, col_k, col_v):
    bsz, t, _ = proj.shape
    w = SWA_BLOCK
    qw = q_heads * SWA_HEAD_DIM
    kw = kv_heads * SWA_HEAD_DIM
    prev = lambda i: jnp.maximum(i - 1, 0)
    kern = functools.partial(_swa_kernel, kv_heads=kv_heads)
    tab_p = pl.BlockSpec((w, LANES), lambda b, i: (b * (t // w) + prev(i), 0))
    tab_c = pl.BlockSpec((w, LANES), lambda b, i: (b * (t // w) + i, 0))
    return pl.pallas_call(
        kern,
        grid=(bsz, t // w),
        in_specs=[
            pl.BlockSpec((1, w, qw), lambda b, i: (b, i, col_q // qw)),
            pl.BlockSpec((1, w, kw), lambda b, i: (b, prev(i), col_k // kw)),
            pl.BlockSpec((1, w, kw), lambda b, i: (b, i, col_k // kw)),
            pl.BlockSpec((1, w, kw), lambda b, i: (b, prev(i), col_v // kw)),
            pl.BlockSpec((1, w, kw), lambda b, i: (b, i, col_v // kw)),
            tab_p, tab_p, tab_c, tab_c,
            pl.BlockSpec((1, qw), lambda b, i: (0, 0)),
        ],
        out_specs=pl.BlockSpec((1, w, qw), lambda b, i: (b, i, 0)),
        out_shape=jax.ShapeDtypeStruct((bsz, t, qw), BF16),
        compiler_params=_cparams(("parallel", "arbitrary")),
        name="swa",
    )(proj, proj, proj, proj, proj, cos, sin, cos, sin, sinks_lane)


def _router_kernel(h_ref, g_ref, wr_ref, br_ref, u_ref, rt_ref):
    h = h_ref[...]
    ms = jnp.mean(h * h, axis=-1, keepdims=True)
    u = h * lax.rsqrt(ms + RMS_EPS) * g_ref[...]
    u_ref[...] = u
    u_hi = u.astype(BF16)
    u_lo = (u - u_hi.astype(F32)).astype(BF16)
    both = jnp.dot(u_hi, wr_ref[...], preferred_element_type=F32)
    logits = (both[:, :LANES] + both[:, LANES:]
              + jnp.dot(u_lo, wr_ref[:, :LANES], preferred_element_type=F32) + br_ref[...])
    lane = lax.broadcasted_iota(I32, logits.shape, 1)
    is_group = lane < N_GROUPS
    gl = jnp.where(is_group, logits, NEG_BIG)
    gmax = jnp.max(gl, axis=-1, keepdims=True)
    gidx = jnp.min(jnp.where(gl == gmax, lane, LANES), axis=-1, keepdims=True)
    gsum = jnp.sum(jnp.where(is_group, jnp.exp(gl - gmax), 0.0), axis=-1, keepdims=True)
    g_w = 1.0 / gsum
    lo = N_GROUPS + gidx * EXPERTS_PER_GROUP
    sel = (lane >= lo) & (lane < lo + EXPERTS_PER_GROUP)
    el = jnp.where(sel, logits, NEG_BIG)
    m1 = jnp.max(el, axis=-1, keepdims=True)
    i1 = jnp.min(jnp.where(el == m1, lane, LANES), axis=-1, keepdims=True)
    el2 = jnp.where(lane == i1, NEG_BIG, el)
    m2 = jnp.max(el2, axis=-1, keepdims=True)
    i2 = jnp.min(jnp.where(el2 == m2, lane, LANES), axis=-1, keepdims=True)
    t = jnp.exp(m2 - m1)
    w1 = g_w / (1.0 + t)
    w2 = g_w * t / (1.0 + t)
    e1 = (i1 - N_GROUPS).astype(F32)
    e2 = (i2 - N_GROUPS).astype(F32)
    rt_ref[...] = jnp.where(lane == 0, e1,
                  jnp.where(lane == 1, e2,
                  jnp.where(lane == 2, w1,
                  jnp.where(lane == 3, w2, 0.0))))


def _router(h2d, g, wr, br, *, rows):
    n, d = h2d.shape
    return pl.pallas_call(
        _router_kernel,
        grid=(n // rows,),
        in_specs=[pl.BlockSpec((rows, d), lambda i: (i, 0)),
                  pl.BlockSpec((1, d), lambda i: (0, 0)),
                  pl.BlockSpec((d, 2 * LANES), lambda i: (0, 0)),
                  pl.BlockSpec((1, LANES), lambda i: (0, 0))],
        out_specs=[pl.BlockSpec((rows, d), lambda i: (i, 0)),
                   pl.BlockSpec((rows, LANES), lambda i: (i, 0))],
        out_shape=[jax.ShapeDtypeStruct((n, d), F32),
                   jax.ShapeDtypeStruct((n, LANES), F32)],
        compiler_params=_cparams(("parallel",)),
        name="router",
    )(h2d, g, wr, br)


def _expert_kernel(te_ref, nu_ref, tv_ref, src_ref, nxt_ref, dst_ref, u_ref, wg_ref, wu_ref,
                   wd_ref, y_ref, xbuf, ybuf, xb_ref, gsem, ssem, *, tm):
    i = pl.program_id(0)
    part = pl.program_id(1)
    n_used = nu_ref[0]

    def for_rows(n_valid, issue):
        groups = n_valid // ROW_DMA_UNROLL

        def group(g, carry):
            for j in range(ROW_DMA_UNROLL):
                issue(g * ROW_DMA_UNROLL + j)
            return carry

        def single(k, carry):
            issue(k)
            return carry

        lax.fori_loop(0, groups, group, 0)
        lax.fori_loop(groups * ROW_DMA_UNROLL, n_valid, single, 0)

    def gather(idx_ref, slot, n_valid):
        def issue(k):
            t = idx_ref[0, 0, k]
            pltpu.make_async_copy(u_ref.at[pl.ds(t, 1), :],
                                  xbuf.at[slot, pl.ds(k, 1), :], gsem.at[slot]).start()
        for_rows(n_valid, issue)

    def scatter(slot, n_valid):
        def issue(k):
            r = dst_ref[0, 0, k]
            pltpu.make_async_copy(ybuf.at[slot, pl.ds(k, 1), :],
                                  y_ref.at[pl.ds(r, 1), :], ssem.at[slot]).start()
        for_rows(n_valid, issue)

    def rows_wait(copy_of, n_valid):
        bit = tm
        while bit >= 1:
            @pl.when((n_valid & bit) != 0)
            def _(bit=bit):
                copy_of(bit).wait()
            bit //= 2

    def gather_wait(slot, n_valid):
        rows_wait(lambda n: pltpu.make_async_copy(
            u_ref.at[pl.ds(0, n), :], xbuf.at[slot, pl.ds(0, n), :], gsem.at[slot]), n_valid)

    def scatter_wait(slot, n_valid):
        rows_wait(lambda n: pltpu.make_async_copy(
            ybuf.at[slot, pl.ds(0, n), :], y_ref.at[pl.ds(0, n), :], ssem.at[slot]), n_valid)

    @pl.when(i < n_used)
    def _():
        slot = i % 2

        @pl.when(part == 0)
        def _():
            @pl.when(i == 0)
            def _():
                xbuf[...] = jnp.zeros_like(xbuf)
                gather(src_ref, 0, tv_ref[0])

            @pl.when(i + 1 < n_used)
            def _():
                gather(nxt_ref, 1 - slot, tv_ref[i + 1])

            gather_wait(slot, tv_ref[i])

            @pl.when(i >= 2)
            def _():
                scatter_wait(slot, tv_ref[i - 2])

            xb_ref[...] = xbuf[slot].astype(BF16)

        x = xb_ref[...]
        a = jnp.dot(x, wg_ref[0].astype(BF16), preferred_element_type=F32)
        b = jnp.dot(x, wu_ref[0].astype(BF16), preferred_element_type=F32)
        hid = (a / (1.0 + jnp.exp(-a)) * b).astype(BF16)
        y = jnp.dot(hid, wd_ref[0].astype(BF16), preferred_element_type=F32)

        @pl.when(part == 0)
        def _():
            ybuf[slot] = y

        @pl.when(part == 1)
        def _():
            ybuf[slot] = ybuf[slot] + y
            scatter(slot, tv_ref[i])

            @pl.when(i == n_used - 1)
            def _():
                scatter_wait(slot, tv_ref[i])

                @pl.when(i >= 1)
                def _():
                    scatter_wait(1 - slot, tv_ref[i - 1])


def _experts(tile_expert, n_used, tile_valid, src_tok, out_row, u2, wg, wu, wd, n_out_rows,
             *, tm):
    n_tiles = src_tok.shape[0]
    _, d = u2.shape
    _, _, f = wg.shape
    kern = functools.partial(_expert_kernel, tm=tm)
    idx_spec = lambda off: pl.BlockSpec(
        (1, 1, tm), lambda i, p, te, nu, tv: (jnp.minimum(i + off, n_tiles - 1), 0, 0),
        memory_space=pltpu.SMEM)
    def half(i, p, nu):
        last = nu[0] - 1
        return jnp.where(i <= last, (i + p) % 2, (last + 1) % 2)
    fh = f // 2
    grid_spec = pltpu.PrefetchScalarGridSpec(
        num_scalar_prefetch=3,
        grid=(n_tiles, 2),
        in_specs=[idx_spec(0), idx_spec(1), idx_spec(0),
                  pl.BlockSpec(memory_space=pl.ANY),
                  pl.BlockSpec((1, d, fh), lambda i, p, te, nu, tv: (te[i], 0, half(i, p, nu))),
                  pl.BlockSpec((1, d, fh), lambda i, p, te, nu, tv: (te[i], 0, half(i, p, nu))),
                  pl.BlockSpec((1, fh, d), lambda i, p, te, nu, tv: (te[i], half(i, p, nu), 0))],
        out_specs=pl.BlockSpec(memory_space=pl.ANY),
        scratch_shapes=[pltpu.VMEM((2, tm, d), F32), pltpu.VMEM((2, tm, d), F32),
                        pltpu.VMEM((tm, d), BF16),
                        pltpu.SemaphoreType.DMA((2,)), pltpu.SemaphoreType.DMA((2,))],
    )
    return pl.pallas_call(
        kern,
        grid_spec=grid_spec,
        out_shape=jax.ShapeDtypeStruct((n_out_rows, d), F32),
        compiler_params=_cparams(("arbitrary", "arbitrary")),
        name="experts",
    )(tile_expert, n_used, tile_valid, src_tok, src_tok, out_row, u2, wg, wu, wd)


def _combine_kernel(h_ref, y0_ref, y1_ref, rt_ref, g_ref, o_ref, *, final_norm):
    rt = rt_ref[...]
    w0 = rt[:, 2:3]
    w1 = rt[:, 3:4]
    h = h_ref[...] + w0 * y0_ref[...] + w1 * y1_ref[...]
    if final_norm:
        ms = jnp.mean(h * h, axis=-1, keepdims=True)
        h = h * lax.rsqrt(ms + RMS_EPS) * g_ref[...]
    o_ref[...] = h


def _combine(h2d, y, route, g, *, rows, final_norm):
    n, d = h2d.shape
    nb = n // rows
    return pl.pallas_call(
        functools.partial(_combine_kernel, final_norm=final_norm),
        grid=(nb,),
        in_specs=[pl.BlockSpec((rows, d), lambda i: (i, 0)),
                  pl.BlockSpec((rows, d), lambda i: (i, 0)),
                  pl.BlockSpec((rows, d), lambda i: (nb + i, 0)),
                  pl.BlockSpec((rows, LANES), lambda i: (i, 0)),
                  pl.BlockSpec((1, d), lambda i: (0, 0))],
        out_specs=pl.BlockSpec((rows, d), lambda i: (i, 0)),
        out_shape=jax.ShapeDtypeStruct((n, d), F32),
        compiler_params=_cparams(("parallel",)),
        name="combine",
    )(h2d, y, y, route, g)


def _dispatch_plan(eid, n_experts, tm, p_rows):
    n = eid.shape[0]
    n_flat = TOP_K * n
    n_tiles = p_rows // tm
    e_flat = eid.T.reshape(-1)
    onehot = (e_flat[:, None] == jnp.arange(n_experts, dtype=I32)[None, :]).astype(I32)
    csum = jnp.cumsum(onehot, axis=0)
    counts = csum[-1]
    pcounts = ((counts + tm - 1) // tm) * tm
    pend = jnp.cumsum(pcounts)
    pstart = pend - pcounts
    dest = jnp.sum(onehot * (csum - 1 + pstart[None, :]), axis=1)
    n_used = (pend[-1] // tm).astype(I32)
    inv = jnp.full((p_rows,), n_flat, I32).at[dest].set(
        jnp.arange(n_flat, dtype=I32), unique_indices=True)
    is_pad = inv >= n_flat
    src_tok = jnp.where(is_pad, 0, inv % n)
    out_row = jnp.where(is_pad, 0, inv)
    tile_start = jnp.arange(n_tiles, dtype=I32) * tm
    used_start = jnp.minimum(tile_start, pend[-1] - tm)
    tile_onehot = ((used_start[:, None] >= pstart[None, :])
                   & (used_start[:, None] < pend[None, :])).astype(I32)
    tile_expert = jnp.sum(tile_onehot * jnp.arange(n_experts, dtype=I32)[None, :], axis=1)
    valid_end = jnp.sum(tile_onehot * (pstart + counts)[None, :], axis=1)
    tile_valid = jnp.where(tile_start < pend[-1], jnp.clip(valid_end - tile_start, 0, tm), 0)
    return (src_tok.reshape(n_tiles, 1, tm), out_row.reshape(n_tiles, 1, tm),
            tile_expert, n_used.reshape(1), tile_valid.astype(I32))


def _tiles(n_tokens, d_model):
    return dict(
        norm_rows=256,
        wide_rows=512,
        in_tm=2048, in_tn=512,
        out_tm=1024, out_tn=1024,
        gla_tc=512,
        moe_tm=256,
    )


def kernel(x, positions, norm_mix_g, w_in, gla_gate_w2, gla_gate_b, gla_out_norm_g,
           swa_sinks, w_out, norm_ffn_g, router_group_w, router_group_b,
           router_expert_w, router_expert_b, expert_w_gate, expert_w_up,
           expert_w_down, norm_final_g):
    bsz, t, d = x.shape
    n = bsz * t
    depth = w_in.shape[0]
    cfg = _tiles(n, d)

    rank = gla_gate_w2.shape[1]
    key_w = gla_gate_w2.shape[2]
    val_w = GLA_HEADS * gla_out_norm_g.shape[1]
    dk, dv = key_w // GLA_HEADS, val_w // GLA_HEADS
    q_heads = swa_sinks.shape[1]
    kv_heads = q_heads // SWA_GROUP
    sq_w, skv_w = q_heads * SWA_HEAD_DIM, kv_heads * SWA_HEAD_DIM
    n_experts = expert_w_gate.shape[1]
    tm = cfg["moe_tm"]
    p_rows = TOP_K * n + n_experts * tm
    for rows in (cfg["in_tm"], cfg["out_tm"], cfg["wide_rows"], cfg["norm_rows"]):
        assert n % rows == 0, (n, rows)
    assert t % cfg["gla_tc"] == 0 and t % SWA_BLOCK == 0, t

    c_gq, c_gk, c_gv, c_gr = 0, key_w, 2 * key_w, 2 * key_w + val_w
    c_ga = 2 * key_w + 2 * val_w
    c_sq = c_ga + rank
    m_sq = c_ga
    m_sk, m_sv = m_sq + sq_w, m_sq + sq_w + skv_w

    half = SWA_HEAD_DIM // 2
    inv_freq = ROPE_THETA ** (-jnp.arange(half, dtype=F32) / half)
    freq_lane = jnp.tile(inv_freq, LANES // half).reshape(1, LANES)
    posf = jnp.broadcast_to(positions.astype(F32).reshape(n, 1), (n, LANES))
    cos_t, sin_t = _rope_tables(posf, freq_lane, rows=1024)

    h2d = x.reshape(n, d)
    for l in range(depth):
        w_bf = w_in[l].astype(BF16)
        w_swa = w_bf[:, c_sq:]
        w_ga = jnp.pad(w_bf[:, c_ga:c_sq], ((0, 0), (0, LANES - rank)))
        w2p = jnp.pad(gla_gate_w2[l], ((0, LANES - rank), (0, 0))).astype(BF16)

        u, ga = _norm_gate(h2d, norm_mix_g[l].reshape(1, d), w_ga, rows=cfg["wide_rows"])
        proj = _in_proj(u, w_bf, w_swa, c_ga, tm=cfg["in_tm"], tn=cfg["in_tn"])
        proj3 = proj.reshape(bsz, t, -1)

        o_gla = _gla(proj3, ga.reshape(bsz, t, LANES), w2p,
                     gla_gate_b[l].reshape(1, key_w), gla_out_norm_g[l].reshape(1, dv),
                     heads=GLA_HEADS, dk=dk, dv=dv, tc=cfg["gla_tc"],
                     col_q=c_gq, col_k=c_gk, col_v=c_gv, col_r=c_gr)
        sinks_lane = jnp.repeat(swa_sinks[l], SWA_HEAD_DIM).reshape(1, sq_w)
        o_swa = _swa(proj3, cos_t, sin_t, sinks_lane, q_heads=q_heads, kv_heads=kv_heads,
                     col_q=m_sq, col_k=m_sk, col_v=m_sv)

        h2d = _out_proj(o_gla.reshape(n, val_w), o_swa.reshape(n, sq_w),
                        w_out[l].astype(BF16), h2d, tm=cfg["out_tm"], tn=cfg["out_tn"])

        wr = jnp.concatenate(
            [router_group_w[l],
             jnp.transpose(router_expert_w[l], (1, 0, 2)).reshape(d, n_experts)], axis=1)
        n_router = wr.shape[1]
        wr = jnp.pad(wr, ((0, 0), (0, LANES - n_router)))
        wr_hi = wr.astype(BF16)
        wr = jnp.concatenate([wr_hi, (wr - wr_hi.astype(F32)).astype(BF16)], axis=1)
        br = jnp.pad(jnp.concatenate([router_group_b[l], router_expert_b[l].reshape(-1)]),
                     (0, LANES - n_router)).reshape(1, LANES)
        u2, route = _router(h2d, norm_ffn_g[l].reshape(1, d), wr, br, rows=cfg["wide_rows"])

        eid = route[:, :TOP_K].astype(I32)
        src_tok, out_row, tile_expert, n_used, tile_valid = _dispatch_plan(
            eid, n_experts, tm, p_rows)
        y_tok = _experts(tile_expert, n_used, tile_valid, src_tok, out_row, u2,
                         expert_w_gate[l], expert_w_up[l], expert_w_down[l],
                         TOP_K * n, tm=tm)

        h2d = _combine(h2d, y_tok, route, norm_final_g.reshape(1, d),
                       rows=cfg["norm_rows"], final_norm=(l == depth - 1))
    return h2d.reshape(bsz, t, d)
```

```python
import functools

import jax
import jax.numpy as jnp
from jax import lax
from jax.experimental import pallas as pl
from jax.experimental.pallas import tpu as pltpu

F32 = jnp.float32
BF16 = jnp.bfloat16
I32 = jnp.int32

RMS_EPS = 1e-6

GLA_HEADS = 8
GLA_GATE_TAU = 16.0
GLA_CHUNK = 64
GLA_HEADS_PER_STEP = 8
SWA_HEAD_DIM = 64
SWA_GROUP = 8
SWA_BLOCK = 128
ROPE_THETA = 10000.0
N_GROUPS = 8
EXPERTS_PER_GROUP = 8
TOP_K = 2
ROW_DMA_UNROLL = 16

LANES = 128
VMEM_LIMIT_BYTES = 56 * 1024 * 1024

NEG_BIG = -1e30


def _cparams(semantics):
    return pltpu.CompilerParams(dimension_semantics=semantics,
                                vmem_limit_bytes=VMEM_LIMIT_BYTES)


def _norm_gate_kernel(x_ref, g_ref, wga_ref, u_ref, ga_ref):
    x = x_ref[...]
    ms = jnp.mean(x * x, axis=-1, keepdims=True)
    u = (x * lax.rsqrt(ms + RMS_EPS) * g_ref[...]).astype(BF16)
    u_ref[...] = u
    ga_ref[...] = jnp.dot(u, wga_ref[...], preferred_element_type=F32)


def _norm_gate(x2d, g, wga, *, rows):
    n, d = x2d.shape
    return pl.pallas_call(
        _norm_gate_kernel,
        grid=(n // rows,),
        in_specs=[pl.BlockSpec((rows, d), lambda i: (i, 0)),
                  pl.BlockSpec((1, d), lambda i: (0, 0)),
                  pl.BlockSpec((d, LANES), lambda i: (0, 0))],
        out_specs=[pl.BlockSpec((rows, d), lambda i: (i, 0)),
                   pl.BlockSpec((rows, LANES), lambda i: (i, 0))],
        out_shape=[jax.ShapeDtypeStruct((n, d), BF16),
                   jax.ShapeDtypeStruct((n, LANES), F32)],
        compiler_params=_cparams(("parallel",)),
        name="norm_gate",
    )(x2d, g, wga)


def _in_proj_kernel(a_ref, wa_ref, wb_ref, o_ref, *, n_a):
    j = pl.program_id(1)

    @pl.when(j < n_a)
    def _():
        o_ref[...] = jnp.dot(a_ref[...], wa_ref[...],
                             preferred_element_type=F32).astype(o_ref.dtype)

    @pl.when(j >= n_a)
    def _():
        o_ref[...] = jnp.dot(a_ref[...], wb_ref[...],
                             preferred_element_type=F32).astype(o_ref.dtype)


def _in_proj(a, wa, wb, n_a_cols, *, tm, tn):
    m, k = a.shape
    n_a = n_a_cols // tn
    n_b = wb.shape[1] // tn
    return pl.pallas_call(
        functools.partial(_in_proj_kernel, n_a=n_a),
        grid=(m // tm, n_a + n_b),
        in_specs=[pl.BlockSpec((tm, k), lambda i, j: (i, 0)),
                  pl.BlockSpec((k, tn), lambda i, j: (0, jnp.minimum(j, n_a - 1))),
                  pl.BlockSpec((k, tn), lambda i, j: (0, jnp.maximum(j - n_a, 0)))],
        out_specs=pl.BlockSpec((tm, tn), lambda i, j: (i, j)),
        out_shape=jax.ShapeDtypeStruct((m, n_a_cols + wb.shape[1]), BF16),
        compiler_params=_cparams(("parallel", "arbitrary")),
        name="in_proj",
    )(a, wa, wb)


def _out_proj_kernel(a_ref, b_ref, wa_ref, wb_ref, x_ref, o_ref):
    acc = jnp.dot(a_ref[...], wa_ref[...], preferred_element_type=F32)
    acc = acc + jnp.dot(b_ref[...], wb_ref[...], preferred_element_type=F32)
    o_ref[...] = x_ref[...] + acc


def _out_proj(a, b, w, x2d, *, tm, tn):
    m, ka = a.shape
    _, kb = b.shape
    _, n = w.shape
    return pl.pallas_call(
        _out_proj_kernel,
        grid=(m // tm, n // tn),
        in_specs=[pl.BlockSpec((tm, ka), lambda i, j: (i, 0)),
                  pl.BlockSpec((tm, kb), lambda i, j: (i, 0)),
                  pl.BlockSpec((ka, tn), lambda i, j: (0, j)),
                  pl.BlockSpec((kb, tn), lambda i, j: (ka // kb, j)),
                  pl.BlockSpec((tm, tn), lambda i, j: (i, j))],
        out_specs=pl.BlockSpec((tm, tn), lambda i, j: (i, j)),
        out_shape=jax.ShapeDtypeStruct((m, n), F32),
        compiler_params=_cparams(("parallel", "arbitrary")),
        name="out_proj",
    )(a, b, w, w, x2d)


def _gla_kernel(q_ref, k_ref, v_ref, r_ref, ga_ref, w2_ref, gb_ref, gn_ref, tri_ref,
                o_ref, st_ref, *, chunk, dk, dv):
    @pl.when(pl.program_id(2) == 0)
    def _():
        st_ref[...] = jnp.zeros_like(st_ref)

    tc = q_ref.shape[1]
    n_chunks = tc // chunk
    heads = range(q_ref.shape[2] // dk)
    rows = lambda a, c: a[c * chunk:(c + 1) * chunk, :]
    last = lambda a, c: a[(c + 1) * chunk - 1:(c + 1) * chunk, :]
    contract_last = (((1,), (1,)), ((), ()))
    contract_first = (((0,), (0,)), ((), ()))
    key = lambda h: slice(h * dk, (h + 1) * dk)
    val = lambda h: slice(h * dv, (h + 1) * dv)

    tri = tri_ref[...]
    tri_b = tri.astype(BF16)
    ga = ga_ref[0].astype(BF16)
    k, v, b = [], [], []
    for h in heads:
        k.append(k_ref[0, :, key(h)].astype(F32))
        v.append(v_ref[0, :, val(h)])
        z = jnp.dot(ga, w2_ref[:, key(h)], preferred_element_type=F32) + gb_ref[:, key(h)]
        g = -(jnp.maximum(-z, 0.0) + jnp.log1p(jnp.exp(-jnp.abs(z)))) / GLA_GATE_TAU
        g_hi = g.astype(BF16)
        g_lo = (g - g_hi.astype(F32)).astype(BF16)
        b.append(jnp.dot(tri_b, g_hi, preferred_element_type=F32)
                 + jnp.dot(tri_b, g_lo, preferred_element_type=F32))
    q_in, o = [], []
    for h in heads:
        q = q_ref[0, :, key(h)].astype(F32) * (dk ** -0.5)
        q_in.append((q * jnp.exp(b[h])).astype(BF16))
        k_in = (k[h] * jnp.exp(-b[h])).astype(BF16)
        a = lax.dot_general(q_in[h], k_in, contract_last, preferred_element_type=F32)
        a = jnp.where(tri > 0.0, a, 0.0).astype(BF16)
        o.append(jnp.dot(a, v[h], preferred_element_type=F32))

    kv_t = [[lax.dot_general(
        rows(v[h], c),
        (rows(k[h], c) * jnp.exp(last(b[h], c) - rows(b[h], c))).astype(BF16),
        contract_first, preferred_element_type=F32) for c in range(n_chunks)] for h in heads]
    st_in = []
    for h in heads:
        st = st_ref[h]
        st_in.append([])
        for c in range(n_chunks):
            st_in[h].append(st.astype(BF16))
            st = st * jnp.exp(last(b[h], c)) + kv_t[h][c]
        st_ref[h] = st
    for h in heads:
        o_inter = [lax.dot_general(rows(q_in[h], c), st_in[h][c], contract_last,
                                   preferred_element_type=F32) for c in range(n_chunks)]
        oh = o[h] + jnp.concatenate(o_inter, axis=0)
        ms = jnp.mean(oh * oh, axis=-1, keepdims=True)
        oh = oh * lax.rsqrt(ms + RMS_EPS) * gn_ref[...]
        r = r_ref[0, :, val(h)].astype(F32)
        oh = oh * (r / (1.0 + jnp.exp(-r)))
        o_ref[0, :, val(h)] = oh.astype(o_ref.dtype)


def _gla(proj, ga, w2p, gb, gn, *, heads, dk, dv, tc, col_q, col_k, col_v, col_r):
    bsz, t, _ = proj.shape
    kern = functools.partial(_gla_kernel, chunk=GLA_CHUNK, dk=dk, dv=dv)
    idx = jnp.arange(tc, dtype=I32)
    tri = ((idx[:, None] // GLA_CHUNK == idx[None, :] // GLA_CHUNK)
           & (idx[None, :] <= idx[:, None])).astype(F32)
    hp = GLA_HEADS_PER_STEP
    kw, vw = hp * dk, hp * dv
    return pl.pallas_call(
        kern,
        grid=(bsz, heads // hp, t // tc),
        in_specs=[
            pl.BlockSpec((1, tc, kw), lambda b, h, i: (b, i, col_q // kw + h)),
            pl.BlockSpec((1, tc, kw), lambda b, h, i: (b, i, col_k // kw + h)),
            pl.BlockSpec((1, tc, vw), lambda b, h, i: (b, i, col_v // vw + h)),
            pl.BlockSpec((1, tc, vw), lambda b, h, i: (b, i, col_r // vw + h)),
            pl.BlockSpec((1, tc, LANES), lambda b, h, i: (b, i, 0)),
            pl.BlockSpec((LANES, kw), lambda b, h, i: (0, h)),
            pl.BlockSpec((1, kw), lambda b, h, i: (0, h)),
            pl.BlockSpec((1, dv), lambda b, h, i: (0, 0)),
            pl.BlockSpec((tc, tc), lambda b, h, i: (0, 0)),
        ],
        out_specs=pl.BlockSpec((1, tc, vw), lambda b, h, i: (b, i, h)),
        out_shape=jax.ShapeDtypeStruct((bsz, t, heads * dv), BF16),
        scratch_shapes=[pltpu.VMEM((hp, dv, dk), F32)],
        compiler_params=_cparams(("parallel", "parallel", "arbitrary")),
        name="gla",
    )(proj, proj, proj, proj, ga, w2p, gb, gn, tri)


def _rope_table_kernel(pos_ref, freq_ref, cos_ref, sin_ref):
    ang = pos_ref[...] * freq_ref[...]
    cos_ref[...] = jnp.cos(ang)
    sin_ref[...] = jnp.sin(ang)


def _rope_tables(posf, freq, *, rows):
    n = posf.shape[0]
    spec = pl.BlockSpec((rows, LANES), lambda i: (i, 0))
    return pl.pallas_call(
        _rope_table_kernel,
        grid=(n // rows,),
        in_specs=[spec, pl.BlockSpec((1, LANES), lambda i: (0, 0))],
        out_specs=[spec, spec],
        out_shape=[jax.ShapeDtypeStruct((n, LANES), F32)] * 2,
        compiler_params=_cparams(("parallel",)),
        name="rope_tables",
    )(posf, freq)


def _rotate_half(x, first_half):
    half = SWA_HEAD_DIM // 2
    width = x.shape[-1]
    fwd = pltpu.roll(x, width - half, 1)
    bwd = pltpu.roll(x, half, 1)
    return jnp.where(first_half, -fwd, bwd)


def _swa_kernel(q_ref, kp_ref, kc_ref, vp_ref, vc_ref, cp_ref, sp_ref, cc_ref,
                sc_ref, sink_ref, o_ref, *, kv_heads):
    w = SWA_BLOCK
    d = SWA_HEAD_DIM
    blk = pl.program_id(1)
    lane = lax.broadcasted_iota(I32, (w, LANES), 1)
    first_half = (lane % d) < (d // 2)
    low_head = lane < d

    def rope(x, cos, sin, fh):
        return x * cos + _rotate_half(x, fh) * sin

    cos_c, sin_c = cc_ref[...], sc_ref[...]
    cos_p, sin_p = cp_ref[...], sp_ref[...]

    kw = kv_heads * d
    lane_k = lax.broadcasted_iota(I32, (w, kw), 1)
    fh_k = (lane_k % d) < (d // 2)
    reps = kw // LANES
    tile = lambda t: jnp.concatenate([t] * reps, axis=1) if reps > 1 else t
    k_prev = rope(kp_ref[0].astype(F32), tile(cos_p), tile(sin_p), fh_k)
    k_cur = rope(kc_ref[0].astype(F32), tile(cos_c), tile(sin_c), fh_k)
    k_all = jnp.concatenate([k_prev, k_cur], axis=0)
    v_all = jnp.concatenate([vp_ref[0], vc_ref[0]], axis=0)

    pairs_per_kv = SWA_GROUP // 2
    m_rows = pairs_per_kv * w
    qr = lax.broadcasted_iota(I32, (m_rows, w), 0) % w
    kc = lax.broadcasted_iota(I32, (m_rows, w), 1)
    from_prev = kc > qr
    no_prev = from_prev & (blk == 0)
    lane_v = lax.broadcasted_iota(I32, (2 * w, LANES), 1)

    scale = d ** -0.5
    chains = []
    for g in range(kv_heads):
        kg = k_all[:, g * d:(g + 1) * d]
        vg = v_all[:, g * d:(g + 1) * d]
        zk = jnp.zeros_like(kg)
        zv = jnp.zeros_like(vg)
        k_lo = jnp.concatenate([kg, zk], axis=1).astype(BF16)
        k_hi = jnp.concatenate([zk, kg], axis=1).astype(BF16)
        v_lo = jnp.where(lane_v == d, 1.0, jnp.concatenate([vg, zv], axis=1))
        v_hi = jnp.where(lane_v == 0, 1.0, jnp.concatenate([zv, vg], axis=1))
        q_parts, sink_lo, sink_hi = [], [], []
        for p in range(pairs_per_kv):
            cb = g * pairs_per_kv + p
            q2 = q_ref[0, :, cb * LANES:(cb + 1) * LANES].astype(F32)
            q_parts.append((rope(q2, cos_c, sin_c, first_half) * scale).astype(BF16))
            sink2 = sink_ref[:, cb * LANES:(cb + 1) * LANES]
            for sel, dst in ((low_head, sink_lo), (~low_head, sink_hi)):
                s1 = jnp.max(jnp.where(sel[:1], sink2, NEG_BIG), axis=-1, keepdims=True)
                dst.append(jnp.broadcast_to(s1, (w, 1)))
        q_st = jnp.concatenate(q_parts, axis=0)
        chains.append((q_st, k_lo, v_lo, jnp.concatenate(sink_lo, axis=0), d))
        chains.append((q_st, k_hi, v_hi, jnp.concatenate(sink_hi, axis=0), 0))

    scores = [lax.dot_general(q_st, kk, (((1,), (1,)), ((), ())), preferred_element_type=F32)
              for q_st, kk, _, _, _ in chains]
    probs, row_max = [], []
    for s, (_, _, _, sink, _) in zip(scores, chains):
        s = jnp.where(from_prev, s[:, :w], s[:, w:])
        s = jnp.where(no_prev, NEG_BIG, s)
        m = jnp.maximum(jnp.max(s, axis=-1, keepdims=True), sink)
        p_un = jnp.exp(s - m)
        probs.append(jnp.concatenate([jnp.where(from_prev, p_un, 0.0),
                                      jnp.where(from_prev, 0.0, p_un)], axis=1).astype(BF16))
        row_max.append(m)
    pv = [jnp.dot(p2, vv, preferred_element_type=F32)
          for p2, (_, _, vv, _, _) in zip(probs, chains)]
    outs = []
    for o, m, (_, _, _, sink, sum_lane) in zip(pv, row_max, chains):
        denom = o[:, sum_lane:sum_lane + 1] + jnp.exp(sink - m)
        outs.append(o / denom)
    low_rows = lax.broadcasted_iota(I32, (m_rows, LANES), 1) < d
    for g in range(kv_heads):
        acc = jnp.where(low_rows, outs[2 * g], outs[2 * g + 1])
        for p in range(pairs_per_kv):
            cb = g * pairs_per_kv + p
            o_ref[0, :, cb * LANES:(cb + 1) * LANES] = acc[p * w:(p + 1) * w].astype(o_ref.dtype)


def _swa(proj, cos, sin, sinks_lane, *, q_heads, kv_heads, col_q, col_k, col_v):
    bsz, t, _ = proj.shape
    w = SWA_BLOCK
    qw = q_heads * SWA_HEAD_DIM
    kw = kv_heads * SWA_HEAD_DIM
    prev = lambda i: jnp.maximum(i - 1, 0)
    kern = functools.partial(_swa_kernel, kv_heads=kv_heads)
    tab_p = pl.BlockSpec((w, LANES), lambda b, i: (b * (t // w) + prev(i), 0))
    tab_c = pl.BlockSpec((w, LANES), lambda b, i: (b * (t // w) + i, 0))
    return pl.pallas_call(
        kern,
        grid=(bsz, t // w),
        in_specs=[
            pl.BlockSpec((1, w, qw), lambda b, i: (b, i, col_q // qw)),
            pl.BlockSpec((1, w, kw), lambda b, i: (b, prev(i), col_k // kw)),
            pl.BlockSpec((1, w, kw), lambda b, i: (b, i, col_k // kw)),
            pl.BlockSpec((1, w, kw), lambda b, i: (b, prev(i), col_v // kw)),
            pl.BlockSpec((1, w, kw), lambda b, i: (b, i, col_v // kw)),
            tab_p, tab_p, tab_c, tab_c,
            pl.BlockSpec((1, qw), lambda b, i: (0, 0)),
        ],
        out_specs=pl.BlockSpec((1, w, qw), lambda b, i: (b, i, 0)),
        out_shape=jax.ShapeDtypeStruct((bsz, t, qw), BF16),
        compiler_params=_cparams(("parallel", "arbitrary")),
        name="swa",
    )(proj, proj, proj, proj, proj, cos, sin, cos, sin, sinks_lane)


def _router_kernel(h_ref, g_ref, wr_ref, br_ref, u_ref, rt_ref):
    h = h_ref[...]
    ms = jnp.mean(h * h, axis=-1, keepdims=True)
    u = h * lax.rsqrt(ms + RMS_EPS) * g_ref[...]
    u_ref[...] = u
    u_hi = u.astype(BF16)
    u_lo = (u - u_hi.astype(F32)).astype(BF16)
    both = jnp.dot(u_hi, wr_ref[...], preferred_element_type=F32)
    logits = (both[:, :LANES] + both[:, LANES:]
              + jnp.dot(u_lo, wr_ref[:, :LANES], preferred_element_type=F32) + br_ref[...])
    lane = lax.broadcasted_iota(I32, logits.shape, 1)
    is_group = lane < N_GROUPS
    gl = jnp.where(is_group, logits, NEG_BIG)
    gmax = jnp.max(gl, axis=-1, keepdims=True)
    gidx = jnp.min(jnp.where(gl == gmax, lane, LANES), axis=-1, keepdims=True)
    gsum = jnp.sum(jnp.where(is_group, jnp.exp(gl - gmax), 0.0), axis=-1, keepdims=True)
    g_w = 1.0 / gsum
    lo = N_GROUPS + gidx * EXPERTS_PER_GROUP
    sel = (lane >= lo) & (lane < lo + EXPERTS_PER_GROUP)
    el = jnp.where(sel, logits, NEG_BIG)
    m1 = jnp.max(el, axis=-1, keepdims=True)
    i1 = jnp.min(jnp.where(el == m1, lane, LANES), axis=-1, keepdims=True)
    el2 = jnp.where(lane == i1, NEG_BIG, el)
    m2 = jnp.max(el2, axis=-1, keepdims=True)
    i2 = jnp.min(jnp.where(el2 == m2, lane, LANES), axis=-1, keepdims=True)
    t = jnp.exp(m2 - m1)
    w1 = g_w / (1.0 + t)
    w2 = g_w * t / (1.0 + t)
    e1 = (i1 - N_GROUPS).astype(F32)
    e2 = (i2 - N_GROUPS).astype(F32)
    rt_ref[...] = jnp.where(lane == 0, e1,
                  jnp.where(lane == 1, e2,
                  jnp.where(lane == 2, w1,
                  jnp.where(lane == 3, w2, 0.0))))


def _router(h2d, g, wr, br, *, rows):
    n, d = h2d.shape
    return pl.pallas_call(
        _router_kernel,
        grid=(n // rows,),
        in_specs=[pl.BlockSpec((rows, d), lambda i: (i, 0)),
                  pl.BlockSpec((1, d), lambda i: (0, 0)),
                  pl.BlockSpec((d, 2 * LANES), lambda i: (0, 0)),
                  pl.BlockSpec((1, LANES), lambda i: (0, 0))],
        out_specs=[pl.BlockSpec((rows, d), lambda i: (i, 0)),
                   pl.BlockSpec((rows, LANES), lambda i: (i, 0))],
        out_shape=[jax.ShapeDtypeStruct((n, d), F32),
                   jax.ShapeDtypeStruct((n, LANES), F32)],
        compiler_params=_cparams(("parallel",)),
        name="router",
    )(h2d, g, wr, br)


def _expert_kernel(te_ref, nu_ref, tv_ref, src_ref, nxt_ref, dst_ref, u_ref, wg_ref, wu_ref,
                   wd_ref, y_ref, xbuf, ybuf, xb_ref, gsem, ssem, *, tm):
    i = pl.program_id(0)
    part = pl.program_id(1)
    n_used = nu_ref[0]

    def for_rows(n_valid, issue):
        groups = n_valid // ROW_DMA_UNROLL

        def group(g, carry):
            for j in range(ROW_DMA_UNROLL):
                issue(g * ROW_DMA_UNROLL + j)
            return carry

        def single(k, carry):
            issue(k)
            return carry

        lax.fori_loop(0, groups, group, 0)
        lax.fori_loop(groups * ROW_DMA_UNROLL, n_valid, single, 0)

    def gather(idx_ref, slot, n_valid):
        def issue(k):
            t = idx_ref[0, 0, k]
            pltpu.make_async_copy(u_ref.at[pl.ds(t, 1), :],
                                  xbuf.at[slot, pl.ds(k, 1), :], gsem.at[slot]).start()
        for_rows(n_valid, issue)

    def scatter(slot, n_valid):
        def issue(k):
            r = dst_ref[0, 0, k]
            pltpu.make_async_copy(ybuf.at[slot, pl.ds(k, 1), :],
                                  y_ref.at[pl.ds(r, 1), :], ssem.at[slot]).start()
        for_rows(n_valid, issue)

    def rows_wait(copy_of, n_valid):
        bit = tm
        while bit >= 1:
            @pl.when((n_valid & bit) != 0)
            def _(bit=bit):
                copy_of(bit).wait()
            bit //= 2

    def gather_wait(slot, n_valid):
        rows_wait(lambda n: pltpu.make_async_copy(
            u_ref.at[pl.ds(0, n), :], xbuf.at[slot, pl.ds(0, n), :], gsem.at[slot]), n_valid)

    def scatter_wait(slot, n_valid):
        rows_wait(lambda n: pltpu.make_async_copy(
            ybuf.at[slot, pl.ds(0, n), :], y_ref.at[pl.ds(0, n), :], ssem.at[slot]), n_valid)

    @pl.when(i < n_used)
    def _():
        slot = i % 2

        @pl.when(part == 0)
        def _():
            @pl.when(i == 0)
            def _():
                xbuf[...] = jnp.zeros_like(xbuf)
                gather(src_ref, 0, tv_ref[0])

            @pl.when(i + 1 < n_used)
            def _():
                gather(nxt_ref, 1 - slot, tv_ref[i + 1])

            gather_wait(slot, tv_ref[i])

            @pl.when(i >= 2)
            def _():
                scatter_wait(slot, tv_ref[i - 2])

            xb_ref[...] = xbuf[slot].astype(BF16)

        x = xb_ref[...]
        a = jnp.dot(x, wg_ref[0].astype(BF16), preferred_element_type=F32)
        b = jnp.dot(x, wu_ref[0].astype(BF16), preferred_element_type=F32)
        hid = (a / (1.0 + jnp.exp(-a)) * b).astype(BF16)
        y = jnp.dot(hid, wd_ref[0].astype(BF16), preferred_element_type=F32)

        @pl.when(part == 0)
        def _():
            ybuf[slot] = y

        @pl.when(part == 1)
        def _():
            ybuf[slot] = ybuf[slot] + y
            scatter(slot, tv_ref[i])

            @pl.when(i == n_used - 1)
            def _():
                scatter_wait(slot, tv_ref[i])

                @pl.when(i >= 1)
                def _():
                    scatter_wait(1 - slot, tv_ref[i - 1])


def _experts(tile_expert, n_used, tile_valid, src_tok, out_row, u2, wg, wu, wd, n_out_rows,
             *, tm):
    n_tiles = src_tok.shape[0]
    _, d = u2.shape
    _, _, f = wg.shape
    kern = functools.partial(_expert_kernel, tm=tm)
    idx_spec = lambda off: pl.BlockSpec(
        (1, 1, tm), lambda i, p, te, nu, tv: (jnp.minimum(i + off, n_tiles - 1), 0, 0),
        memory_space=pltpu.SMEM)
    def half(i, p, nu):
        last = nu[0] - 1
        return jnp.where(i <= last, (i + p) % 2, (last + 1) % 2)
    fh = f // 2
    grid_spec = pltpu.PrefetchScalarGridSpec(
        num_scalar_prefetch=3,
        grid=(n_tiles, 2),
        in_specs=[idx_spec(0), idx_spec(1), idx_spec(0),
                  pl.BlockSpec(memory_space=pl.ANY),
                  pl.BlockSpec((1, d, fh), lambda i, p, te, nu, tv: (te[i], 0, half(i, p, nu))),
                  pl.BlockSpec((1, d, fh), lambda i, p, te, nu, tv: (te[i], 0, half(i, p, nu))),
                  pl.BlockSpec((1, fh, d), lambda i, p, te, nu, tv: (te[i], half(i, p, nu), 0))],
        out_specs=pl.BlockSpec(memory_space=pl.ANY),
        scratch_shapes=[pltpu.VMEM((2, tm, d), F32), pltpu.VMEM((2, tm, d), F32),
                        pltpu.VMEM((tm, d), BF16),
                        pltpu.SemaphoreType.DMA((2,)), pltpu.SemaphoreType.DMA((2,))],
    )
    return pl.pallas_call(
        kern,
        grid_spec=grid_spec,
        out_shape=jax.ShapeDtypeStruct((n_out_rows, d), F32),
        compiler_params=_cparams(("arbitrary", "arbitrary")),
        name="experts",
    )(tile_expert, n_used, tile_valid, src_tok, src_tok, out_row, u2, wg, wu, wd)


def _combine_kernel(h_ref, y0_ref, y1_ref, rt_ref, g_ref, o_ref, *, final_norm):
    rt = rt_ref[...]
    w0 = rt[:, 2:3]
    w1 = rt[:, 3:4]
    h = h_ref[...] + w0 * y0_ref[...] + w1 * y1_ref[...]
    if final_norm:
        ms = jnp.mean(h * h, axis=-1, keepdims=True)
        h = h * lax.rsqrt(ms + RMS_EPS) * g_ref[...]
    o_ref[...] = h


def _combine(h2d, y, route, g, *, rows, final_norm):
    n, d = h2d.shape
    nb = n // rows
    return pl.pallas_call(
        functools.partial(_combine_kernel, final_norm=final_norm),
        grid=(nb,),
        in_specs=[pl.BlockSpec((rows, d), lambda i: (i, 0)),
                  pl.BlockSpec((rows, d), lambda i: (i, 0)),
                  pl.BlockSpec((rows, d), lambda i: (nb + i, 0)),
                  pl.BlockSpec((rows, LANES), lambda i: (i, 0)),
                  pl.BlockSpec((1, d), lambda i: (0, 0))],
        out_specs=pl.BlockSpec((rows, d), lambda i: (i, 0)),
        out_shape=jax.ShapeDtypeStruct((n, d), F32),
        compiler_params=_cparams(("parallel",)),
        name="combine",
    )(h2d, y, y, route, g)


def _dispatch_plan(eid, n_experts, tm, p_rows):
    n = eid.shape[0]
    n_flat = TOP_K * n
    n_tiles = p_rows // tm
    e_flat = eid.T.reshape(-1)
    onehot = (e_flat[:, None] == jnp.arange(n_experts, dtype=I32)[None, :]).astype(I32)
    counts = jnp.sum(onehot, axis=0)
    pcounts = ((counts + tm - 1) // tm) * tm
    pend = jnp.cumsum(pcounts)
    pstart = pend - pcounts
    start = jnp.cumsum(counts) - counts
    n_used = (pend[-1] // tm).astype(I32)
    assert n_experts * n_flat < 2 ** 31
    key = e_flat * n_flat + jnp.arange(n_flat, dtype=I32)
    neg_sorted, _ = lax.top_k(-key, n_flat)
    order = (-neg_sorted) % n_flat
    tile_start = jnp.arange(n_tiles, dtype=I32) * tm
    used_start = jnp.minimum(tile_start, pend[-1] - tm)
    tile_onehot = ((used_start[:, None] >= pstart[None, :])
                   & (used_start[:, None] < pend[None, :])).astype(I32)
    tile_expert = jnp.sum(tile_onehot * jnp.arange(n_experts, dtype=I32)[None, :], axis=1)
    valid_end = jnp.sum(tile_onehot * (pstart + counts)[None, :], axis=1)
    tile_valid = jnp.where(tile_start < pend[-1], jnp.clip(valid_end - tile_start, 0, tm), 0)
    first = jnp.sum(tile_onehot * (start - pstart)[None, :], axis=1) + used_start
    k = jnp.arange(tm, dtype=I32)
    entry = order[jnp.clip(first[:, None] + k[None, :], 0, n_flat - 1)]
    valid = k[None, :] < tile_valid[:, None]
    src_tok = jnp.where(valid, entry % n, 0)
    out_row = jnp.where(valid, entry, 0)
    return (src_tok.reshape(n_tiles, 1, tm), out_row.reshape(n_tiles, 1, tm),
            tile_expert, n_used.reshape(1), tile_valid.astype(I32))


def _tiles(n_tokens, d_model):
    return dict(
        norm_rows=256,
        wide_rows=512,
        in_tm=2048, in_tn=512,
        out_tm=1024, out_tn=1024,
        gla_tc=512,
        moe_tm=256,
    )


def kernel(x, positions, norm_mix_g, w_in, gla_gate_w2, gla_gate_b, gla_out_norm_g,
           swa_sinks, w_out, norm_ffn_g, router_group_w, router_group_b,
           router_expert_w, router_expert_b, expert_w_gate, expert_w_up,
           expert_w_down, norm_final_g):
    bsz, t, d = x.shape
    n = bsz * t
    depth = w_in.shape[0]
    cfg = _tiles(n, d)

    rank = gla_gate_w2.shape[1]
    key_w = gla_gate_w2.shape[2]
    val_w = GLA_HEADS * gla_out_norm_g.shape[1]
    dk, dv = key_w // GLA_HEADS, val_w // GLA_HEADS
    q_heads = swa_sinks.shape[1]
    kv_heads = q_heads // SWA_GROUP
    sq_w, skv_w = q_heads * SWA_HEAD_DIM, kv_heads * SWA_HEAD_DIM
    n_experts = expert_w_gate.shape[1]
    tm = cfg["moe_tm"]
    p_rows = TOP_K * n + n_experts * tm
    for rows in (cfg["in_tm"], cfg["out_tm"], cfg["wide_rows"], cfg["norm_rows"]):
        assert n % rows == 0, (n, rows)
    assert t % cfg["gla_tc"] == 0 and t % SWA_BLOCK == 0, t

    c_gq, c_gk, c_gv, c_gr = 0, key_w, 2 * key_w, 2 * key_w + val_w
    c_ga = 2 * key_w + 2 * val_w
    c_sq = c_ga + rank
    m_sq = c_ga
    m_sk, m_sv = m_sq + sq_w, m_sq + sq_w + skv_w

    half = SWA_HEAD_DIM // 2
    inv_freq = ROPE_THETA ** (-jnp.arange(half, dtype=F32) / half)
    freq_lane = jnp.tile(inv_freq, LANES // half).reshape(1, LANES)
    posf = jnp.broadcast_to(positions.astype(F32).reshape(n, 1), (n, LANES))
    cos_t, sin_t = _rope_tables(posf, freq_lane, rows=1024)

    h2d = x.reshape(n, d)
    for l in range(depth):
        w_bf = w_in[l].astype(BF16)
        w_swa = w_bf[:, c_sq:]
        w_ga = jnp.pad(w_bf[:, c_ga:c_sq], ((0, 0), (0, LANES - rank)))
        w2p = jnp.pad(gla_gate_w2[l], ((0, LANES - rank), (0, 0))).astype(BF16)

        u, ga = _norm_gate(h2d, norm_mix_g[l].reshape(1, d), w_ga, rows=cfg["wide_rows"])
        proj = _in_proj(u, w_bf, w_swa, c_ga, tm=cfg["in_tm"], tn=cfg["in_tn"])
        proj3 = proj.reshape(bsz, t, -1)

        o_gla = _gla(proj3, ga.reshape(bsz, t, LANES), w2p,
                     gla_gate_b[l].reshape(1, key_w), gla_out_norm_g[l].reshape(1, dv),
                     heads=GLA_HEADS, dk=dk, dv=dv, tc=cfg["gla_tc"],
                     col_q=c_gq, col_k=c_gk, col_v=c_gv, col_r=c_gr)
        sinks_lane = jnp.repeat(swa_sinks[l], SWA_HEAD_DIM).reshape(1, sq_w)
        o_swa = _swa(proj3, cos_t, sin_t, sinks_lane, q_heads=q_heads, kv_heads=kv_heads,
                     col_q=m_sq, col_k=m_sk, col_v=m_sv)

        h2d = _out_proj(o_gla.reshape(n, val_w), o_swa.reshape(n, sq_w),
                        w_out[l].astype(BF16), h2d, tm=cfg["out_tm"], tn=cfg["out_tn"])

        wr = jnp.concatenate(
            [router_group_w[l],
             jnp.transpose(router_expert_w[l], (1, 0, 2)).reshape(d, n_experts)], axis=1)
        n_router = wr.shape[1]
        wr = jnp.pad(wr, ((0, 0), (0, LANES - n_router)))
        wr_hi = wr.astype(BF16)
        wr = jnp.concatenate([wr_hi, (wr - wr_hi.astype(F32)).astype(BF16)], axis=1)
        br = jnp.pad(jnp.concatenate([router_group_b[l], router_expert_b[l].reshape(-1)]),
                     (0, LANES - n_router)).reshape(1, LANES)
        u2, route = _router(h2d, norm_ffn_g[l].reshape(1, d), wr, br, rows=cfg["wide_rows"])

        eid = route[:, :TOP_K].astype(I32)
        src_tok, out_row, tile_expert, n_used, tile_valid = _dispatch_plan(
            eid, n_experts, tm, p_rows)
        y_tok = _experts(tile_expert, n_used, tile_valid, src_tok, out_row, u2,
                         expert_w_gate[l], expert_w_up[l], expert_w_down[l],
                         TOP_K * n, tm=tm)

        h2d = _combine(h2d, y_tok, route, norm_final_g.reshape(1, d),
                       rows=cfg["norm_rows"], final_norm=(l == depth - 1))
    return h2d.reshape(bsz, t, d)
```

```python
import functools

import jax
import jax.numpy as jnp
from jax import lax
from jax.experimental import pallas as pl
from jax.experimental.pallas import tpu as pltpu

F32 = jnp.float32
BF16 = jnp.bfloat16
I32 = jnp.int32

RMS_EPS = 1e-6

GLA_HEADS = 8
GLA_GATE_TAU = 16.0
GLA_CHUNK = 64
GLA_HEADS_PER_STEP = 8
SWA_HEAD_DIM = 64
SWA_GROUP = 8
SWA_BLOCK = 128
ROPE_THETA = 10000.0
N_GROUPS = 8
EXPERTS_PER_GROUP = 8
TOP_K = 2
ROW_DMA_UNROLL = 16

LANES = 128
VMEM_LIMIT_BYTES = 56 * 1024 * 1024

NEG_BIG = -1e30


def _cparams(semantics):
    return pltpu.CompilerParams(dimension_semantics=semantics,
                                vmem_limit_bytes=VMEM_LIMIT_BYTES)


def _norm_gate_kernel(x_ref, g_ref, wga_ref, u_ref, ga_ref):
    x = x_ref[...]
    ms = jnp.mean(x * x, axis=-1, keepdims=True)
    u = (x * lax.rsqrt(ms + RMS_EPS) * g_ref[...]).astype(BF16)
    u_ref[...] = u
    ga_ref[...] = jnp.dot(u, wga_ref[...], preferred_element_type=F32)


def _norm_gate(x2d, g, wga, *, rows):
    n, d = x2d.shape
    return pl.pallas_call(
        _norm_gate_kernel,
        grid=(n // rows,),
        in_specs=[pl.BlockSpec((rows, d), lambda i: (i, 0)),
                  pl.BlockSpec((1, d), lambda i: (0, 0)),
                  pl.BlockSpec((d, LANES), lambda i: (0, 0))],
        out_specs=[pl.BlockSpec((rows, d), lambda i: (i, 0)),
                   pl.BlockSpec((rows, LANES), lambda i: (i, 0))],
        out_shape=[jax.ShapeDtypeStruct((n, d), BF16),
                   jax.ShapeDtypeStruct((n, LANES), F32)],
        compiler_params=_cparams(("parallel",)),
        name="norm_gate",
    )(x2d, g, wga)


def _in_proj_kernel(a_ref, wa_ref, wb_ref, o_ref, *, n_a):
    j = pl.program_id(1)

    @pl.when(j < n_a)
    def _():
        o_ref[...] = jnp.dot(a_ref[...], wa_ref[...],
                             preferred_element_type=F32).astype(o_ref.dtype)

    @pl.when(j >= n_a)
    def _():
        o_ref[...] = jnp.dot(a_ref[...], wb_ref[...],
                             preferred_element_type=F32).astype(o_ref.dtype)


def _in_proj(a, wa, wb, n_a_cols, *, tm, tn):
    m, k = a.shape
    n_a = n_a_cols // tn
    n_b = wb.shape[1] // tn
    return pl.pallas_call(
        functools.partial(_in_proj_kernel, n_a=n_a),
        grid=(m // tm, n_a + n_b),
        in_specs=[pl.BlockSpec((tm, k), lambda i, j: (i, 0)),
                  pl.BlockSpec((k, tn), lambda i, j: (0, jnp.minimum(j, n_a - 1))),
                  pl.BlockSpec((k, tn), lambda i, j: (0, jnp.maximum(j - n_a, 0)))],
        out_specs=pl.BlockSpec((tm, tn), lambda i, j: (i, j)),
        out_shape=jax.ShapeDtypeStruct((m, n_a_cols + wb.shape[1]), BF16),
        compiler_params=_cparams(("parallel", "arbitrary")),
        name="in_proj",
    )(a, wa, wb)


def _out_proj_kernel(a_ref, b_ref, wa_ref, wb_ref, x_ref, o_ref):
    acc = jnp.dot(a_ref[...], wa_ref[...], preferred_element_type=F32)
    acc = acc + jnp.dot(b_ref[...], wb_ref[...], preferred_element_type=F32)
    o_ref[...] = x_ref[...] + acc


def _out_proj(a, b, w, x2d, *, tm, tn):
    m, ka = a.shape
    _, kb = b.shape
    _, n = w.shape
    return pl.pallas_call(
        _out_proj_kernel,
        grid=(m // tm, n // tn),
        in_specs=[pl.BlockSpec((tm, ka), lambda i, j: (i, 0)),
                  pl.BlockSpec((tm, kb), lambda i, j: (i, 0)),
                  pl.BlockSpec((ka, tn), lambda i, j: (0, j)),
                  pl.BlockSpec((kb, tn), lambda i, j: (ka // kb, j)),
                  pl.BlockSpec((tm, tn), lambda i, j: (i, j))],
        out_specs=pl.BlockSpec((tm, tn), lambda i, j: (i, j)),
        out_shape=jax.ShapeDtypeStruct((m, n), F32),
        compiler_params=_cparams(("parallel", "arbitrary")),
        name="out_proj",
    )(a, b, w, w, x2d)


def _gla_kernel(q_ref, k_ref, v_ref, r_ref, ga_ref, w2_ref, gb_ref, gn_ref, tri_ref,
                o_ref, st_ref, *, chunk, dk, dv):
    @pl.when(pl.program_id(2) == 0)
    def _():
        st_ref[...] = jnp.zeros_like(st_ref)

    tc = q_ref.shape[1]
    n_chunks = tc // chunk
    heads = range(q_ref.shape[2] // dk)
    rows = lambda a, c: a[c * chunk:(c + 1) * chunk, :]
    last = lambda a, c: a[(c + 1) * chunk - 1:(c + 1) * chunk, :]
    contract_last = (((1,), (1,)), ((), ()))
    contract_first = (((0,), (0,)), ((), ()))
    key = lambda h: slice(h * dk, (h + 1) * dk)
    val = lambda h: slice(h * dv, (h + 1) * dv)

    tri = tri_ref[...]
    tri_b = tri.astype(BF16)
    ga = ga_ref[0].astype(BF16)
    k, v, b = [], [], []
    for h in heads:
        k.append(k_ref[0, :, key(h)].astype(F32))
        v.append(v_ref[0, :, val(h)])
        z = jnp.dot(ga, w2_ref[:, key(h)], preferred_element_type=F32) + gb_ref[:, key(h)]
        g = -(jnp.maximum(-z, 0.0) + jnp.log1p(jnp.exp(-jnp.abs(z)))) / GLA_GATE_TAU
        g_hi = g.astype(BF16)
        g_lo = (g - g_hi.astype(F32)).astype(BF16)
        b.append(jnp.dot(tri_b, g_hi, preferred_element_type=F32)
                 + jnp.dot(tri_b, g_lo, preferred_element_type=F32))
    q_in, o = [], []
    for h in heads:
        q = q_ref[0, :, key(h)].astype(F32) * (dk ** -0.5)
        q_in.append((q * jnp.exp(b[h])).astype(BF16))
        k_in = (k[h] * jnp.exp(-b[h])).astype(BF16)
        a = lax.dot_general(q_in[h], k_in, contract_last, preferred_element_type=F32)
        a = jnp.where(tri > 0.0, a, 0.0).astype(BF16)
        o.append(jnp.dot(a, v[h], preferred_element_type=F32))

    kv_t = [[lax.dot_general(
        rows(v[h], c),
        (rows(k[h], c) * jnp.exp(last(b[h], c) - rows(b[h], c))).astype(BF16),
        contract_first, preferred_element_type=F32) for c in range(n_chunks)] for h in heads]
    st_in = []
    for h in heads:
        st = st_ref[h]
        st_in.append([])
        for c in range(n_chunks):
            st_in[h].append(st.astype(BF16))
            st = st * jnp.exp(last(b[h], c)) + kv_t[h][c]
        st_ref[h] = st
    for h in heads:
        o_inter = [lax.dot_general(rows(q_in[h], c), st_in[h][c], contract_last,
                                   preferred_element_type=F32) for c in range(n_chunks)]
        oh = o[h] + jnp.concatenate(o_inter, axis=0)
        ms = jnp.mean(oh * oh, axis=-1, keepdims=True)
        oh = oh * lax.rsqrt(ms + RMS_EPS) * gn_ref[...]
        r = r_ref[0, :, val(h)].astype(F32)
        oh = oh * (r / (1.0 + jnp.exp(-r)))
        o_ref[0, :, val(h)] = oh.astype(o_ref.dtype)


def _gla(proj, ga, w2p, gb, gn, *, heads, dk, dv, tc, col_q, col_k, col_v, col_r):
    bsz, t, _ = proj.shape
    kern = functools.partial(_gla_kernel, chunk=GLA_CHUNK, dk=dk, dv=dv)
    idx = jnp.arange(tc, dtype=I32)
    tri = ((idx[:, None] // GLA_CHUNK == idx[None, :] // GLA_CHUNK)
           & (idx[None, :] <= idx[:, None])).astype(F32)
    hp = GLA_HEADS_PER_STEP
    kw, vw = hp * dk, hp * dv
    return pl.pallas_call(
        kern,
        grid=(bsz, heads // hp, t // tc),
        in_specs=[
            pl.BlockSpec((1, tc, kw), lambda b, h, i: (b, i, col_q // kw + h)),
            pl.BlockSpec((1, tc, kw), lambda b, h, i: (b, i, col_k // kw + h)),
            pl.BlockSpec((1, tc, vw), lambda b, h, i: (b, i, col_v // vw + h)),
            pl.BlockSpec((1, tc, vw), lambda b, h, i: (b, i, col_r // vw + h)),
            pl.BlockSpec((1, tc, LANES), lambda b, h, i: (b, i, 0)),
            pl.BlockSpec((LANES, kw), lambda b, h, i: (0, h)),
            pl.BlockSpec((1, kw), lambda b, h, i: (0, h)),
            pl.BlockSpec((1, dv), lambda b, h, i: (0, 0)),
            pl.BlockSpec((tc, tc), lambda b, h, i: (0, 0)),
        ],
        out_specs=pl.BlockSpec((1, tc, vw), lambda b, h, i: (b, i, h)),
        out_shape=jax.ShapeDtypeStruct((bsz, t, heads * dv), BF16),
        scratch_shapes=[pltpu.VMEM((hp, dv, dk), F32)],
        compiler_params=_cparams(("parallel", "parallel", "arbitrary")),
        name="gla",
    )(proj, proj, proj, proj, ga, w2p, gb, gn, tri)


def _rope_table_kernel(pos_ref, freq_ref, cos_ref, sin_ref):
    ang = pos_ref[...] * freq_ref[...]
    cos_ref[...] = jnp.cos(ang)
    sin_ref[...] = jnp.sin(ang)


def _rope_tables(posf, freq, *, rows):
    n = posf.shape[0]
    spec = pl.BlockSpec((rows, LANES), lambda i: (i, 0))
    return pl.pallas_call(
        _rope_table_kernel,
        grid=(n // rows,),
        in_specs=[spec, pl.BlockSpec((1, LANES), lambda i: (0, 0))],
        out_specs=[spec, spec],
        out_shape=[jax.ShapeDtypeStruct((n, LANES), F32)] * 2,
        compiler_params=_cparams(("parallel",)),
        name="rope_tables",
    )(posf, freq)


def _rotate_half(x, first_half):
    half = SWA_HEAD_DIM // 2
    width = x.shape[-1]
    fwd = pltpu.roll(x, width - half, 1)
    bwd = pltpu.roll(x, half, 1)
    return jnp.where(first_half, -fwd, bwd)


def _swa_kernel(q_ref, kp_ref, kc_ref, vp_ref, vc_ref, cp_ref, sp_ref, cc_ref,
                sc_ref, sink_ref, o_ref, *, kv_heads):
    w = SWA_BLOCK
    d = SWA_HEAD_DIM
    blk = pl.program_id(1)
    lane = lax.broadcasted_iota(I32, (w, LANES), 1)
    first_half = (lane % d) < (d // 2)
    low_head = lane < d

    def rope(x, cos, sin, fh):
        return x * cos + _rotate_half(x, fh) * sin

    cos_c, sin_c = cc_ref[...], sc_ref[...]
    cos_p, sin_p = cp_ref[...], sp_ref[...]

    kw = kv_heads * d
    lane_k = lax.broadcasted_iota(I32, (w, kw), 1)
    fh_k = (lane_k % d) < (d // 2)
    reps = kw // LANES
    tile = lambda t: jnp.concatenate([t] * reps, axis=1) if reps > 1 else t
    k_prev = rope(kp_ref[0].astype(F32), tile(cos_p), tile(sin_p), fh_k)
    k_cur = rope(kc_ref[0].astype(F32), tile(cos_c), tile(sin_c), fh_k)
    k_all = jnp.concatenate([k_prev, k_cur], axis=0)
    v_all = jnp.concatenate([vp_ref[0], vc_ref[0]], axis=0)

    pairs_per_kv = SWA_GROUP // 2
    m_rows = pairs_per_kv * w
    qr = lax.broadcasted_iota(I32, (m_rows, w), 0) % w
    kc = lax.broadcasted_iota(I32, (m_rows, w), 1)
    from_prev = kc > qr
    no_prev = from_prev & (blk == 0)
    lane_v = lax.broadcasted_iota(I32, (2 * w, LANES), 1)

    scale = d ** -0.5
    chains = []
    for g in range(kv_heads):
        kg = k_all[:, g * d:(g + 1) * d]
        vg = v_all[:, g * d:(g + 1) * d]
        zk = jnp.zeros_like(kg)
        zv = jnp.zeros_like(vg)
        k_lo = jnp.concatenate([kg, zk], axis=1).astype(BF16)
        k_hi = jnp.concatenate([zk, kg], axis=1).astype(BF16)
        v_lo = jnp.where(lane_v == d, 1.0, jnp.concatenate([vg, zv], axis=1))
        v_hi = jnp.where(lane_v == 0, 1.0, jnp.concatenate([zv, vg], axis=1))
        q_parts, sink_lo, sink_hi = [], [], []
        for p in range(pairs_per_kv):
            cb = g * pairs_per_kv + p
            q2 = q_ref[0, :, cb * LANES:(cb + 1) * LANES].astype(F32)
            q_parts.append((rope(q2, cos_c, sin_c, first_half) * scale).astype(BF16))
            sink2 = sink_ref[:, cb * LANES:(cb + 1) * LANES]
            for sel, dst in ((low_head, sink_lo), (~low_head, sink_hi)):
                s1 = jnp.max(jnp.where(sel[:1], sink2, NEG_BIG), axis=-1, keepdims=True)
                dst.append(jnp.broadcast_to(s1, (w, 1)))
        q_st = jnp.concatenate(q_parts, axis=0)
        chains.append((q_st, k_lo, v_lo, jnp.concatenate(sink_lo, axis=0), d))
        chains.append((q_st, k_hi, v_hi, jnp.concatenate(sink_hi, axis=0), 0))

    scores = [lax.dot_general(q_st, kk, (((1,), (1,)), ((), ())), preferred_element_type=F32)
              for q_st, kk, _, _, _ in chains]
    probs, row_max = [], []
    for s, (_, _, _, sink, _) in zip(scores, chains):
        s = jnp.where(from_prev, s[:, :w], s[:, w:])
        s = jnp.where(no_prev, NEG_BIG, s)
        m = jnp.maximum(jnp.max(s, axis=-1, keepdims=True), sink)
        p_un = jnp.exp(s - m)
        probs.append(jnp.concatenate([jnp.where(from_prev, p_un, 0.0),
                                      jnp.where(from_prev, 0.0, p_un)], axis=1).astype(BF16))
        row_max.append(m)
    pv = [jnp.dot(p2, vv, preferred_element_type=F32)
          for p2, (_, _, vv, _, _) in zip(probs, chains)]
    outs = []
    for o, m, (_, _, _, sink, sum_lane) in zip(pv, row_max, chains):
        denom = o[:, sum_lane:sum_lane + 1] + jnp.exp(sink - m)
        outs.append(o / denom)
    low_rows = lax.broadcasted_iota(I32, (m_rows, LANES), 1) < d
    for g in range(kv_heads):
        acc = jnp.where(low_rows, outs[2 * g], outs[2 * g + 1])
        for p in range(pairs_per_kv):
            cb = g * pairs_per_kv + p
            o_ref[0, :, cb * LANES:(cb + 1) * LANES] = acc[p * w:(p + 1) * w].astype(o_ref.dtype)


def _swa(proj, cos, sin, sinks_lane, *, q_heads, kv_heads, col_q, col_k, col_v):
    bsz, t, _ = proj.shape
    w = SWA_BLOCK
    qw = q_heads * SWA_HEAD_DIM
    kw = kv_heads * SWA_HEAD_DIM
    prev = lambda i: jnp.maximum(i - 1, 0)
    kern = functools.partial(_swa_kernel, kv_heads=kv_heads)
    tab_p = pl.BlockSpec((w, LANES), lambda b, i: (b * (t // w) + prev(i), 0))
    tab_c = pl.BlockSpec((w, LANES), lambda b, i: (b * (t // w) + i, 0))
    return pl.pallas_call(
        kern,
        grid=(bsz, t // w),
        in_specs=[
            pl.BlockSpec((1, w, qw), lambda b, i: (b, i, col_q // qw)),
            pl.BlockSpec((1, w, kw), lambda b, i: (b, prev(i), col_k // kw)),
            pl.BlockSpec((1, w, kw), lambda b, i: (b, i, col_k // kw)),
            pl.BlockSpec((1, w, kw), lambda b, i: (b, prev(i), col_v // kw)),
            pl.BlockSpec((1, w, kw), lambda b, i: (b, i, col_v // kw)),
            tab_p, tab_p, tab_c, tab_c,
            pl.BlockSpec((1, qw), lambda b, i: (0, 0)),
        ],
        out_specs=pl.BlockSpec((1, w, qw), lambda b, i: (b, i, 0)),
        out_shape=jax.ShapeDtypeStruct((bsz, t, qw), BF16),
        compiler_params=_cparams(("parallel", "arbitrary")),
        name="swa",
    )(proj, proj, proj, proj, proj, cos, sin, cos, sin, sinks_lane)


def _router_kernel(h_ref, g_ref, wr_ref, br_ref, u_ref, rt_ref):
    h = h_ref[...]
    ms = jnp.mean(h * h, axis=-1, keepdims=True)
    u = h * lax.rsqrt(ms + RMS_EPS) * g_ref[...]
    u_ref[...] = u
    u_hi = u.astype(BF16)
    u_lo = (u - u_hi.astype(F32)).astype(BF16)
    both = jnp.dot(u_hi, wr_ref[...], preferred_element_type=F32)
    logits = (both[:, :LANES] + both[:, LANES:]
              + jnp.dot(u_lo, wr_ref[:, :LANES], preferred_element_type=F32) + br_ref[...])
    lane = lax.broadcasted_iota(I32, logits.shape, 1)
    is_group = lane < N_GROUPS
    gl = jnp.where(is_group, logits, NEG_BIG)
    gmax = jnp.max(gl, axis=-1, keepdims=True)
    gidx = jnp.min(jnp.where(gl == gmax, lane, LANES), axis=-1, keepdims=True)
    gsum = jnp.sum(jnp.where(is_group, jnp.exp(gl - gmax), 0.0), axis=-1, keepdims=True)
    g_w = 1.0 / gsum
    lo = N_GROUPS + gidx * EXPERTS_PER_GROUP
    sel = (lane >= lo) & (lane < lo + EXPERTS_PER_GROUP)
    el = jnp.where(sel, logits, NEG_BIG)
    m1 = jnp.max(el, axis=-1, keepdims=True)
    i1 = jnp.min(jnp.where(el == m1, lane, LANES), axis=-1, keepdims=True)
    el2 = jnp.where(lane == i1, NEG_BIG, el)
    m2 = jnp.max(el2, axis=-1, keepdims=True)
    i2 = jnp.min(jnp.where(el2 == m2, lane, LANES), axis=-1, keepdims=True)
    t = jnp.exp(m2 - m1)
    w1 = g_w / (1.0 + t)
    w2 = g_w * t / (1.0 + t)
    e1 = (i1 - N_GROUPS).astype(F32)
    e2 = (i2 - N_GROUPS).astype(F32)
    rt_ref[...] = jnp.where(lane == 0, e1,
                  jnp.where(lane == 1, e2,
                  jnp.where(lane == 2, w1,
                  jnp.where(lane == 3, w2, 0.0))))


def _router(h2d, g, wr, br, *, rows):
    n, d = h2d.shape
    return pl.pallas_call(
        _router_kernel,
        grid=(n // rows,),
        in_specs=[pl.BlockSpec((rows, d), lambda i: (i, 0)),
                  pl.BlockSpec((1, d), lambda i: (0, 0)),
                  pl.BlockSpec((d, 2 * LANES), lambda i: (0, 0)),
                  pl.BlockSpec((1, LANES), lambda i: (0, 0))],
        out_specs=[pl.BlockSpec((rows, d), lambda i: (i, 0)),
                   pl.BlockSpec((rows, LANES), lambda i: (i, 0))],
        out_shape=[jax.ShapeDtypeStruct((n, d), F32),
                   jax.ShapeDtypeStruct((n, LANES), F32)],
        compiler_params=_cparams(("parallel",)),
        name="router",
    )(h2d, g, wr, br)


def _expert_kernel(te_ref, nu_ref, tv_ref, first_ref, order_ref, u_ref, wg_ref, wu_ref,
                   wd_ref, y_ref, xbuf, ybuf, xb_ref, gsem, ssem, *, tm, n_tokens):
    i = pl.program_id(0)
    part = pl.program_id(1)
    n_used = nu_ref[0]

    def for_rows(n_valid, issue):
        groups = n_valid // ROW_DMA_UNROLL

        def group(g, carry):
            for j in range(ROW_DMA_UNROLL):
                issue(g * ROW_DMA_UNROLL + j)
            return carry

        def single(k, carry):
            issue(k)
            return carry

        lax.fori_loop(0, groups, group, 0)
        lax.fori_loop(groups * ROW_DMA_UNROLL, n_valid, single, 0)

    def gather(tile, slot, n_valid):
        base = first_ref[tile]

        def issue(k):
            e = order_ref[base + k]
            t = jnp.where(e >= n_tokens, e - n_tokens, e)
            pltpu.make_async_copy(u_ref.at[pl.ds(t, 1), :],
                                  xbuf.at[slot, pl.ds(k, 1), :], gsem.at[slot]).start()
        for_rows(n_valid, issue)

    def scatter(tile, slot, n_valid):
        base = first_ref[tile]

        def issue(k):
            r = order_ref[base + k]
            pltpu.make_async_copy(ybuf.at[slot, pl.ds(k, 1), :],
                                  y_ref.at[pl.ds(r, 1), :], ssem.at[slot]).start()
        for_rows(n_valid, issue)

    def rows_wait(copy_of, n_valid):
        bit = tm
        while bit >= 1:
            @pl.when((n_valid & bit) != 0)
            def _(bit=bit):
                copy_of(bit).wait()
            bit //= 2

    def gather_wait(slot, n_valid):
        rows_wait(lambda n: pltpu.make_async_copy(
            u_ref.at[pl.ds(0, n), :], xbuf.at[slot, pl.ds(0, n), :], gsem.at[slot]), n_valid)

    def scatter_wait(slot, n_valid):
        rows_wait(lambda n: pltpu.make_async_copy(
            ybuf.at[slot, pl.ds(0, n), :], y_ref.at[pl.ds(0, n), :], ssem.at[slot]), n_valid)

    @pl.when(i < n_used)
    def _():
        slot = i % 2

        @pl.when(part == 0)
        def _():
            @pl.when(i == 0)
            def _():
                xbuf[...] = jnp.zeros_like(xbuf)
                gather(0, 0, tv_ref[0])

            @pl.when(i + 1 < n_used)
            def _():
                gather(i + 1, 1 - slot, tv_ref[i + 1])

            gather_wait(slot, tv_ref[i])

            @pl.when(i >= 2)
            def _():
                scatter_wait(slot, tv_ref[i - 2])

            xb_ref[...] = xbuf[slot].astype(BF16)

        x = xb_ref[...]
        a = jnp.dot(x, wg_ref[0].astype(BF16), preferred_element_type=F32)
        b = jnp.dot(x, wu_ref[0].astype(BF16), preferred_element_type=F32)
        hid = (a / (1.0 + jnp.exp(-a)) * b).astype(BF16)
        y = jnp.dot(hid, wd_ref[0].astype(BF16), preferred_element_type=F32)

        @pl.when(part == 0)
        def _():
            ybuf[slot] = y

        @pl.when(part == 1)
        def _():
            ybuf[slot] = ybuf[slot] + y
            scatter(i, slot, tv_ref[i])

            @pl.when(i == n_used - 1)
            def _():
                scatter_wait(slot, tv_ref[i])

                @pl.when(i >= 1)
                def _():
                    scatter_wait(1 - slot, tv_ref[i - 1])


def _experts(tile_expert, n_used, tile_valid, first, order, u2, wg, wu, wd, *, tm):
    n_tiles = tile_expert.shape[0]
    n_tokens, d = u2.shape
    _, _, f = wg.shape
    kern = functools.partial(_expert_kernel, tm=tm, n_tokens=n_tokens)
    def half(i, p, nu):
        last = nu[0] - 1
        return jnp.where(i <= last, (i + p) % 2, (last + 1) % 2)
    fh = f // 2
    grid_spec = pltpu.PrefetchScalarGridSpec(
        num_scalar_prefetch=5,
        grid=(n_tiles, 2),
        in_specs=[pl.BlockSpec(memory_space=pl.ANY),
                  pl.BlockSpec((1, d, fh),
                               lambda i, p, te, nu, tv, fi, od: (te[i], 0, half(i, p, nu))),
                  pl.BlockSpec((1, d, fh),
                               lambda i, p, te, nu, tv, fi, od: (te[i], 0, half(i, p, nu))),
                  pl.BlockSpec((1, fh, d),
                               lambda i, p, te, nu, tv, fi, od: (te[i], half(i, p, nu), 0))],
        out_specs=pl.BlockSpec(memory_space=pl.ANY),
        scratch_shapes=[pltpu.VMEM((2, tm, d), F32), pltpu.VMEM((2, tm, d), F32),
                        pltpu.VMEM((tm, d), BF16),
                        pltpu.SemaphoreType.DMA((2,)), pltpu.SemaphoreType.DMA((2,))],
    )
    return pl.pallas_call(
        kern,
        grid_spec=grid_spec,
        out_shape=jax.ShapeDtypeStruct((order.shape[0], d), F32),
        compiler_params=_cparams(("arbitrary", "arbitrary")),
        name="experts",
    )(tile_expert, n_used, tile_valid, first, order, u2, wg, wu, wd)


def _combine_kernel(h_ref, y0_ref, y1_ref, rt_ref, g_ref, o_ref, *, final_norm):
    rt = rt_ref[...]
    w0 = rt[:, 2:3]
    w1 = rt[:, 3:4]
    h = h_ref[...] + w0 * y0_ref[...] + w1 * y1_ref[...]
    if final_norm:
        ms = jnp.mean(h * h, axis=-1, keepdims=True)
        h = h * lax.rsqrt(ms + RMS_EPS) * g_ref[...]
    o_ref[...] = h


def _combine(h2d, y, route, g, *, rows, final_norm):
    n, d = h2d.shape
    nb = n // rows
    return pl.pallas_call(
        functools.partial(_combine_kernel, final_norm=final_norm),
        grid=(nb,),
        in_specs=[pl.BlockSpec((rows, d), lambda i: (i, 0)),
                  pl.BlockSpec((rows, d), lambda i: (i, 0)),
                  pl.BlockSpec((rows, d), lambda i: (nb + i, 0)),
                  pl.BlockSpec((rows, LANES), lambda i: (i, 0)),
                  pl.BlockSpec((1, d), lambda i: (0, 0))],
        out_specs=pl.BlockSpec((rows, d), lambda i: (i, 0)),
        out_shape=jax.ShapeDtypeStruct((n, d), F32),
        compiler_params=_cparams(("parallel",)),
        name="combine",
    )(h2d, y, y, route, g)


def _dispatch_plan(eid, n_experts, tm, p_rows):
    assert TOP_K == 2
    n = eid.shape[0]
    n_flat = TOP_K * n
    n_tiles = p_rows // tm
    e_flat = eid.T.reshape(-1)
    onehot = (e_flat[:, None] == jnp.arange(n_experts, dtype=I32)[None, :]).astype(I32)
    counts = jnp.sum(onehot, axis=0)
    pcounts = ((counts + tm - 1) // tm) * tm
    pend = jnp.cumsum(pcounts)
    pstart = pend - pcounts
    start = jnp.cumsum(counts) - counts
    n_used = (pend[-1] // tm).astype(I32)
    assert n_experts * n_flat < 2 ** 31
    key = e_flat * n_flat + jnp.arange(n_flat, dtype=I32)
    neg_sorted, _ = lax.top_k(-key, n_flat)
    order = (-neg_sorted) % n_flat
    tile_start = jnp.arange(n_tiles, dtype=I32) * tm
    used_start = jnp.minimum(tile_start, pend[-1] - tm)
    tile_onehot = ((used_start[:, None] >= pstart[None, :])
                   & (used_start[:, None] < pend[None, :])).astype(I32)
    tile_expert = jnp.sum(tile_onehot * jnp.arange(n_experts, dtype=I32)[None, :], axis=1)
    valid_end = jnp.sum(tile_onehot * (pstart + counts)[None, :], axis=1)
    tile_valid = jnp.where(tile_start < pend[-1], jnp.clip(valid_end - tile_start, 0, tm), 0)
    first = jnp.sum(tile_onehot * (start - pstart)[None, :], axis=1) + used_start
    return (order.astype(I32), first.astype(I32), tile_expert, n_used.reshape(1),
            tile_valid.astype(I32))


def _tiles(n_tokens, d_model):
    return dict(
        norm_rows=256,
        wide_rows=512,
        in_tm=2048, in_tn=512,
        out_tm=1024, out_tn=1024,
        gla_tc=512,
        moe_tm=256,
    )


def kernel(x, positions, norm_mix_g, w_in, gla_gate_w2, gla_gate_b, gla_out_norm_g,
           swa_sinks, w_out, norm_ffn_g, router_group_w, router_group_b,
           router_expert_w, router_expert_b, expert_w_gate, expert_w_up,
           expert_w_down, norm_final_g):
    bsz, t, d = x.shape
    n = bsz * t
    depth = w_in.shape[0]
    cfg = _tiles(n, d)

    rank = gla_gate_w2.shape[1]
    key_w = gla_gate_w2.shape[2]
    val_w = GLA_HEADS * gla_out_norm_g.shape[1]
    dk, dv = key_w // GLA_HEADS, val_w // GLA_HEADS
    q_heads = swa_sinks.shape[1]
    kv_heads = q_heads // SWA_GROUP
    sq_w, skv_w = q_heads * SWA_HEAD_DIM, kv_heads * SWA_HEAD_DIM
    n_experts = expert_w_gate.shape[1]
    tm = cfg["moe_tm"]
    p_rows = TOP_K * n + n_experts * tm
    for rows in (cfg["in_tm"], cfg["out_tm"], cfg["wide_rows"], cfg["norm_rows"]):
        assert n % rows == 0, (n, rows)
    assert t % cfg["gla_tc"] == 0 and t % SWA_BLOCK == 0, t

    c_gq, c_gk, c_gv, c_gr = 0, key_w, 2 * key_w, 2 * key_w + val_w
    c_ga = 2 * key_w + 2 * val_w
    c_sq = c_ga + rank
    m_sq = c_ga
    m_sk, m_sv = m_sq + sq_w, m_sq + sq_w + skv_w

    half = SWA_HEAD_DIM // 2
    inv_freq = ROPE_THETA ** (-jnp.arange(half, dtype=F32) / half)
    freq_lane = jnp.tile(inv_freq, LANES // half).reshape(1, LANES)
    posf = jnp.broadcast_to(positions.astype(F32).reshape(n, 1), (n, LANES))
    cos_t, sin_t = _rope_tables(posf, freq_lane, rows=1024)

    h2d = x.reshape(n, d)
    for l in range(depth):
        w_bf = w_in[l].astype(BF16)
        w_swa = w_bf[:, c_sq:]
        w_ga = jnp.pad(w_bf[:, c_ga:c_sq], ((0, 0), (0, LANES - rank)))
        w2p = jnp.pad(gla_gate_w2[l], ((0, LANES - rank), (0, 0))).astype(BF16)

        u, ga = _norm_gate(h2d, norm_mix_g[l].reshape(1, d), w_ga, rows=cfg["wide_rows"])
        proj = _in_proj(u, w_bf, w_swa, c_ga, tm=cfg["in_tm"], tn=cfg["in_tn"])
        proj3 = proj.reshape(bsz, t, -1)

        o_gla = _gla(proj3, ga.reshape(bsz, t, LANES), w2p,
                     gla_gate_b[l].reshape(1, key_w), gla_out_norm_g[l].reshape(1, dv),
                     heads=GLA_HEADS, dk=dk, dv=dv, tc=cfg["gla_tc"],
                     col_q=c_gq, col_k=c_gk, col_v=c_gv, col_r=c_gr)
        sinks_lane = jnp.repeat(swa_sinks[l], SWA_HEAD_DIM).reshape(1, sq_w)
        o_swa = _swa(proj3, cos_t, sin_t, sinks_lane, q_heads=q_heads, kv_heads=kv_heads,
                     col_q=m_sq, col_k=m_sk, col_v=m_sv)

        h2d = _out_proj(o_gla.reshape(n, val_w), o_swa.reshape(n, sq_w),
                        w_out[l].astype(BF16), h2d, tm=cfg["out_tm"], tn=cfg["out_tn"])

        wr = jnp.concatenate(
            [router_group_w[l],
             jnp.transpose(router_expert_w[l], (1, 0, 2)).reshape(d, n_experts)], axis=1)
        n_router = wr.shape[1]
        wr = jnp.pad(wr, ((0, 0), (0, LANES - n_router)))
        wr_hi = wr.astype(BF16)
        wr = jnp.concatenate([wr_hi, (wr - wr_hi.astype(F32)).astype(BF16)], axis=1)
        br = jnp.pad(jnp.concatenate([router_group_b[l], router_expert_b[l].reshape(-1)]),
                     (0, LANES - n_router)).reshape(1, LANES)
        u2, route = _router(h2d, norm_ffn_g[l].reshape(1, d), wr, br, rows=cfg["wide_rows"])

        eid = route[:, :TOP_K].astype(I32)
        order, first, tile_expert, n_used, tile_valid = _dispatch_plan(
            eid, n_experts, tm, p_rows)
        y_tok = _experts(tile_expert, n_used, tile_valid, first, order, u2,
                         expert_w_gate[l], expert_w_up[l], expert_w_down[l], tm=tm)

        h2d = _combine(h2d, y_tok, route, norm_final_g.reshape(1, d),
                       rows=cfg["norm_rows"], final_norm=(l == depth - 1))
    return h2d.reshape(bsz, t, d)
```

```python
import functools

import jax
import jax.numpy as jnp
from jax import lax
from jax.experimental import pallas as pl
from jax.experimental.pallas import tpu as pltpu

F32 = jnp.float32
BF16 = jnp.bfloat16
I32 = jnp.int32

RMS_EPS = 1e-6

GLA_HEADS = 8
GLA_GATE_TAU = 16.0
GLA_CHUNK = 64
GLA_HEADS_PER_STEP = 8
SWA_HEAD_DIM = 64
SWA_GROUP = 8
SWA_BLOCK = 128
ROPE_THETA = 10000.0
N_GROUPS = 8
EXPERTS_PER_GROUP = 8
TOP_K = 2
ROW_DMA_UNROLL = 16

LANES = 128
VMEM_LIMIT_BYTES = 56 * 1024 * 1024

NEG_BIG = -1e30


def _cparams(semantics):
    return pltpu.CompilerParams(dimension_semantics=semantics,
                                vmem_limit_bytes=VMEM_LIMIT_BYTES)


def _norm_gate_kernel(x_ref, g_ref, wga_ref, u_ref, ga_ref):
    x = x_ref[...]
    ms = jnp.mean(x * x, axis=-1, keepdims=True)
    u = (x * lax.rsqrt(ms + RMS_EPS) * g_ref[...]).astype(BF16)
    u_ref[...] = u
    ga_ref[...] = jnp.dot(u, wga_ref[...], preferred_element_type=F32)


def _norm_gate(x2d, g, wga, *, rows):
    n, d = x2d.shape
    return pl.pallas_call(
        _norm_gate_kernel,
        grid=(n // rows,),
        in_specs=[pl.BlockSpec((rows, d), lambda i: (i, 0)),
                  pl.BlockSpec((1, d), lambda i: (0, 0)),
                  pl.BlockSpec((d, LANES), lambda i: (0, 0))],
        out_specs=[pl.BlockSpec((rows, d), lambda i: (i, 0)),
                   pl.BlockSpec((rows, LANES), lambda i: (i, 0))],
        out_shape=[jax.ShapeDtypeStruct((n, d), BF16),
                   jax.ShapeDtypeStruct((n, LANES), F32)],
        compiler_params=_cparams(("parallel",)),
        name="norm_gate",
    )(x2d, g, wga)


def _in_proj_kernel(a_ref, wa_ref, wb_ref, o_ref, *, n_a):
    j = pl.program_id(1)

    @pl.when(j < n_a)
    def _():
        o_ref[...] = jnp.dot(a_ref[...], wa_ref[...],
                             preferred_element_type=F32).astype(o_ref.dtype)

    @pl.when(j >= n_a)
    def _():
        o_ref[...] = jnp.dot(a_ref[...], wb_ref[...],
                             preferred_element_type=F32).astype(o_ref.dtype)


def _in_proj(a, wa, wb, n_a_cols, *, tm, tn):
    m, k = a.shape
    n_a = n_a_cols // tn
    n_b = wb.shape[1] // tn
    return pl.pallas_call(
        functools.partial(_in_proj_kernel, n_a=n_a),
        grid=(m // tm, n_a + n_b),
        in_specs=[pl.BlockSpec((tm, k), lambda i, j: (i, 0)),
                  pl.BlockSpec((k, tn), lambda i, j: (0, jnp.minimum(j, n_a - 1))),
                  pl.BlockSpec((k, tn), lambda i, j: (0, jnp.maximum(j - n_a, 0)))],
        out_specs=pl.BlockSpec((tm, tn), lambda i, j: (i, j)),
        out_shape=jax.ShapeDtypeStruct((m, n_a_cols + wb.shape[1]), BF16),
        compiler_params=_cparams(("parallel", "arbitrary")),
        name="in_proj",
    )(a, wa, wb)


def _out_proj_kernel(a_ref, b_ref, wa_ref, wb_ref, x_ref, o_ref):
    acc = jnp.dot(a_ref[...], wa_ref[...], preferred_element_type=F32)
    acc = acc + jnp.dot(b_ref[...], wb_ref[...], preferred_element_type=F32)
    o_ref[...] = x_ref[...] + acc


def _out_proj(a, b, w, x2d, *, tm, tn):
    m, ka = a.shape
    _, kb = b.shape
    _, n = w.shape
    return pl.pallas_call(
        _out_proj_kernel,
        grid=(m // tm, n // tn),
        in_specs=[pl.BlockSpec((tm, ka), lambda i, j: (i, 0)),
                  pl.BlockSpec((tm, kb), lambda i, j: (i, 0)),
                  pl.BlockSpec((ka, tn), lambda i, j: (0, j)),
                  pl.BlockSpec((kb, tn), lambda i, j: (ka // kb, j)),
                  pl.BlockSpec((tm, tn), lambda i, j: (i, j))],
        out_specs=pl.BlockSpec((tm, tn), lambda i, j: (i, j)),
        out_shape=jax.ShapeDtypeStruct((m, n), F32),
        compiler_params=_cparams(("parallel", "arbitrary")),
        name="out_proj",
    )(a, b, w, w, x2d)


def _gla_kernel(q_ref, k_ref, v_ref, r_ref, ga_ref, w2_ref, gb_ref, gn_ref, tri_ref,
                o_ref, st_ref, *, chunk, dk, dv):
    @pl.when(pl.program_id(2) == 0)
    def _():
        st_ref[...] = jnp.zeros_like(st_ref)

    tc = q_ref.shape[1]
    n_chunks = tc // chunk
    heads = range(q_ref.shape[2] // dk)
    rows = lambda a, c: a[c * chunk:(c + 1) * chunk, :]
    last = lambda a, c: a[(c + 1) * chunk - 1:(c + 1) * chunk, :]
    contract_last = (((1,), (1,)), ((), ()))
    contract_first = (((0,), (0,)), ((), ()))
    key = lambda h: slice(h * dk, (h + 1) * dk)
    val = lambda h: slice(h * dv, (h + 1) * dv)

    tri = tri_ref[...]
    tri_b = tri.astype(BF16)
    ga = ga_ref[0].astype(BF16)
    k, v, b = [], [], []
    for h in heads:
        k.append(k_ref[0, :, key(h)].astype(F32))
        v.append(v_ref[0, :, val(h)])
        z = jnp.dot(ga, w2_ref[:, key(h)], preferred_element_type=F32) + gb_ref[:, key(h)]
        g = -(jnp.maximum(-z, 0.0) + jnp.log1p(jnp.exp(-jnp.abs(z)))) / GLA_GATE_TAU
        g_hi = g.astype(BF16)
        g_lo = (g - g_hi.astype(F32)).astype(BF16)
        b.append(jnp.dot(tri_b, g_hi, preferred_element_type=F32)
                 + jnp.dot(tri_b, g_lo, preferred_element_type=F32))
    q_in, o = [], []
    for h in heads:
        q = q_ref[0, :, key(h)].astype(F32) * (dk ** -0.5)
        q_in.append((q * jnp.exp(b[h])).astype(BF16))
        k_in = (k[h] * jnp.exp(-b[h])).astype(BF16)
        a = lax.dot_general(q_in[h], k_in, contract_last, preferred_element_type=F32)
        a = jnp.where(tri > 0.0, a, 0.0).astype(BF16)
        o.append(jnp.dot(a, v[h], preferred_element_type=F32))

    kv_t = [[lax.dot_general(
        rows(v[h], c),
        (rows(k[h], c) * jnp.exp(last(b[h], c) - rows(b[h], c))).astype(BF16),
        contract_first, preferred_element_type=F32) for c in range(n_chunks)] for h in heads]
    st_in = []
    for h in heads:
        st = st_ref[h]
        st_in.append([])
        for c in range(n_chunks):
            st_in[h].append(st.astype(BF16))
            st = st * jnp.exp(last(b[h], c)) + kv_t[h][c]
        st_ref[h] = st
    for h in heads:
        o_inter = [lax.dot_general(rows(q_in[h], c), st_in[h][c], contract_last,
                                   preferred_element_type=F32) for c in range(n_chunks)]
        oh = o[h] + jnp.concatenate(o_inter, axis=0)
        ms = jnp.mean(oh * oh, axis=-1, keepdims=True)
        oh = oh * lax.rsqrt(ms + RMS_EPS) * gn_ref[...]
        r = r_ref[0, :, val(h)].astype(F32)
        oh = oh * (r / (1.0 + jnp.exp(-r)))
        o_ref[0, :, val(h)] = oh.astype(o_ref.dtype)


def _gla(proj, ga, w2p, gb, gn, *, heads, dk, dv, tc, col_q, col_k, col_v, col_r):
    bsz, t, _ = proj.shape
    kern = functools.partial(_gla_kernel, chunk=GLA_CHUNK, dk=dk, dv=dv)
    idx = jnp.arange(tc, dtype=I32)
    tri = ((idx[:, None] // GLA_CHUNK == idx[None, :] // GLA_CHUNK)
           & (idx[None, :] <= idx[:, None])).astype(F32)
    hp = GLA_HEADS_PER_STEP
    kw, vw = hp * dk, hp * dv
    return pl.pallas_call(
        kern,
        grid=(bsz, heads // hp, t // tc),
        in_specs=[
            pl.BlockSpec((1, tc, kw), lambda b, h, i: (b, i, col_q // kw + h)),
            pl.BlockSpec((1, tc, kw), lambda b, h, i: (b, i, col_k // kw + h)),
            pl.BlockSpec((1, tc, vw), lambda b, h, i: (b, i, col_v // vw + h)),
            pl.BlockSpec((1, tc, vw), lambda b, h, i: (b, i, col_r // vw + h)),
            pl.BlockSpec((1, tc, LANES), lambda b, h, i: (b, i, 0)),
            pl.BlockSpec((LANES, kw), lambda b, h, i: (0, h)),
            pl.BlockSpec((1, kw), lambda b, h, i: (0, h)),
            pl.BlockSpec((1, dv), lambda b, h, i: (0, 0)),
            pl.BlockSpec((tc, tc), lambda b, h, i: (0, 0)),
        ],
        out_specs=pl.BlockSpec((1, tc, vw), lambda b, h, i: (b, i, h)),
        out_shape=jax.ShapeDtypeStruct((bsz, t, heads * dv), BF16),
        scratch_shapes=[pltpu.VMEM((hp, dv, dk), F32)],
        compiler_params=_cparams(("parallel", "parallel", "arbitrary")),
        name="gla",
    )(proj, proj, proj, proj, ga, w2p, gb, gn, tri)


def _rope_table_kernel(pos_ref, freq_ref, cos_ref, sin_ref):
    ang = pos_ref[...] * freq_ref[...]
    cos_ref[...] = jnp.cos(ang)
    sin_ref[...] = jnp.sin(ang)


def _rope_tables(posf, freq, *, rows):
    n = posf.shape[0]
    spec = pl.BlockSpec((rows, LANES), lambda i: (i, 0))
    return pl.pallas_call(
        _rope_table_kernel,
        grid=(n // rows,),
        in_specs=[spec, pl.BlockSpec((1, LANES), lambda i: (0, 0))],
        out_specs=[spec, spec],
        out_shape=[jax.ShapeDtypeStruct((n, LANES), F32)] * 2,
        compiler_params=_cparams(("parallel",)),
        name="rope_tables",
    )(posf, freq)


def _rotate_half(x, first_half):
    half = SWA_HEAD_DIM // 2
    width = x.shape[-1]
    fwd = pltpu.roll(x, width - half, 1)
    bwd = pltpu.roll(x, half, 1)
    return jnp.where(first_half, -fwd, bwd)


def _swa_kernel(q_ref, kp_ref, kc_ref, vp_ref, vc_ref, cp_ref, sp_ref, cc_ref,
                sc_ref, sink_ref, o_ref, *, kv_heads):
    w = SWA_BLOCK
    d = SWA_HEAD_DIM
    blk = pl.program_id(1)
    lane = lax.broadcasted_iota(I32, (w, LANES), 1)
    first_half = (lane % d) < (d // 2)
    low_head = lane < d

    def rope(x, cos, sin, fh):
        return x * cos + _rotate_half(x, fh) * sin

    cos_c, sin_c = cc_ref[...], sc_ref[...]
    cos_p, sin_p = cp_ref[...], sp_ref[...]

    kw = kv_heads * d
    lane_k = lax.broadcasted_iota(I32, (w, kw), 1)
    fh_k = (lane_k % d) < (d // 2)
    reps = kw // LANES
    tile = lambda t: jnp.concatenate([t] * reps, axis=1) if reps > 1 else t
    k_prev = rope(kp_ref[0].astype(F32), tile(cos_p), tile(sin_p), fh_k)
    k_cur = rope(kc_ref[0].astype(F32), tile(cos_c), tile(sin_c), fh_k)
    k_all = jnp.concatenate([k_prev, k_cur], axis=0)
    v_all = jnp.concatenate([vp_ref[0], vc_ref[0]], axis=0)

    pairs_per_kv = SWA_GROUP // 2
    m_rows = pairs_per_kv * w
    qr = lax.broadcasted_iota(I32, (m_rows, w), 0) % w
    kc = lax.broadcasted_iota(I32, (m_rows, w), 1)
    from_prev = kc > qr
    no_prev = from_prev & (blk == 0)
    lane_v = lax.broadcasted_iota(I32, (2 * w, LANES), 1)

    scale = d ** -0.5
    cos_q, sin_q = cos_c * scale, sin_c * scale
    chains = []
    for g in range(kv_heads):
        kg = k_all[:, g * d:(g + 1) * d]
        vg = v_all[:, g * d:(g + 1) * d]
        zk = jnp.zeros_like(kg)
        zv = jnp.zeros_like(vg)
        k_lo = jnp.concatenate([kg, zk], axis=1).astype(BF16)
        k_hi = jnp.concatenate([zk, kg], axis=1).astype(BF16)
        v_lo = jnp.where(lane_v == d, 1.0, jnp.concatenate([vg, zv], axis=1))
        v_hi = jnp.where(lane_v == 0, 1.0, jnp.concatenate([zv, vg], axis=1))
        q_parts, sink_lo, sink_hi = [], [], []
        for p in range(pairs_per_kv):
            cb = g * pairs_per_kv + p
            q2 = q_ref[0, :, cb * LANES:(cb + 1) * LANES].astype(F32)
            q_parts.append(rope(q2, cos_q, sin_q, first_half).astype(BF16))
            sink2 = sink_ref[:, cb * LANES:(cb + 1) * LANES]
            for sel, dst in ((low_head, sink_lo), (~low_head, sink_hi)):
                s1 = jnp.max(jnp.where(sel[:1], sink2, NEG_BIG), axis=-1, keepdims=True)
                dst.append(jnp.broadcast_to(s1, (w, 1)))
        q_st = jnp.concatenate(q_parts, axis=0)
        chains.append((q_st, k_lo, v_lo, jnp.concatenate(sink_lo, axis=0), d))
        chains.append((q_st, k_hi, v_hi, jnp.concatenate(sink_hi, axis=0), 0))

    scores = [lax.dot_general(q_st, kk, (((1,), (1,)), ((), ())), preferred_element_type=F32)
              for q_st, kk, _, _, _ in chains]
    probs, row_max = [], []
    for s, (_, _, _, sink, _) in zip(scores, chains):
        s = jnp.where(from_prev, s[:, :w], s[:, w:])
        s = jnp.where(no_prev, NEG_BIG, s)
        m = jnp.maximum(jnp.max(s, axis=-1, keepdims=True), sink)
        p_un = jnp.exp(s - m)
        probs.append(jnp.concatenate([jnp.where(from_prev, p_un, 0.0),
                                      jnp.where(from_prev, 0.0, p_un)], axis=1).astype(BF16))
        row_max.append(m)
    pv = [jnp.dot(p2, vv, preferred_element_type=F32)
          for p2, (_, _, vv, _, _) in zip(probs, chains)]
    outs = []
    for o, m, (_, _, _, sink, sum_lane) in zip(pv, row_max, chains):
        denom = o[:, sum_lane:sum_lane + 1] + jnp.exp(sink - m)
        outs.append(o / denom)
    low_rows = lax.broadcasted_iota(I32, (m_rows, LANES), 1) < d
    for g in range(kv_heads):
        acc = jnp.where(low_rows, outs[2 * g], outs[2 * g + 1])
        for p in range(pairs_per_kv):
            cb = g * pairs_per_kv + p
            o_ref[0, :, cb * LANES:(cb + 1) * LANES] = acc[p * w:(p + 1) * w].astype(o_ref.dtype)


def _swa(proj, cos, sin, sinks_lane, *, q_heads, kv_heads, col_q, col_k, col_v):
    bsz, t, _ = proj.shape
    w = SWA_BLOCK
    qw = q_heads * SWA_HEAD_DIM
    kw = kv_heads * SWA_HEAD_DIM
    prev = lambda i: jnp.maximum(i - 1, 0)
    kern = functools.partial(_swa_kernel, kv_heads=kv_heads)
    tab_p = pl.BlockSpec((w, LANES), lambda b, i: (b * (t // w) + prev(i), 0))
    tab_c = pl.BlockSpec((w, LANES), lambda b, i: (b * (t // w) + i, 0))
    return pl.pallas_call(
        kern,
        grid=(bsz, t // w),
        in_specs=[
            pl.BlockSpec((1, w, qw), lambda b, i: (b, i, col_q // qw)),
            pl.BlockSpec((1, w, kw), lambda b, i: (b, prev(i), col_k // kw)),
            pl.BlockSpec((1, w, kw), lambda b, i: (b, i, col_k // kw)),
            pl.BlockSpec((1, w, kw), lambda b, i: (b, prev(i), col_v // kw)),
            pl.BlockSpec((1, w, kw), lambda b, i: (b, i, col_v // kw)),
            tab_p, tab_p, tab_c, tab_c,
            pl.BlockSpec((1, qw), lambda b, i: (0, 0)),
        ],
        out_specs=pl.BlockSpec((1, w, qw), lambda b, i: (b, i, 0)),
        out_shape=jax.ShapeDtypeStruct((bsz, t, qw), BF16),
        compiler_params=_cparams(("parallel", "arbitrary")),
        name="swa",
    )(proj, proj, proj, proj, proj, cos, sin, cos, sin, sinks_lane)


def _router_kernel(h_ref, g_ref, wr_ref, br_ref, u_ref, rt_ref):
    h = h_ref[...]
    ms = jnp.mean(h * h, axis=-1, keepdims=True)
    u = h * lax.rsqrt(ms + RMS_EPS) * g_ref[...]
    u_ref[...] = u
    u_hi = u.astype(BF16)
    u_lo = (u - u_hi.astype(F32)).astype(BF16)
    both = jnp.dot(u_hi, wr_ref[...], preferred_element_type=F32)
    logits = (both[:, :LANES] + both[:, LANES:]
              + jnp.dot(u_lo, wr_ref[:, :LANES], preferred_element_type=F32) + br_ref[...])
    lane = lax.broadcasted_iota(I32, logits.shape, 1)
    is_group = lane < N_GROUPS
    gl = jnp.where(is_group, logits, NEG_BIG)
    gmax = jnp.max(gl, axis=-1, keepdims=True)
    gidx = jnp.min(jnp.where(gl == gmax, lane, LANES), axis=-1, keepdims=True)
    gsum = jnp.sum(jnp.where(is_group, jnp.exp(gl - gmax), 0.0), axis=-1, keepdims=True)
    g_w = 1.0 / gsum
    lo = N_GROUPS + gidx * EXPERTS_PER_GROUP
    sel = (lane >= lo) & (lane < lo + EXPERTS_PER_GROUP)
    el = jnp.where(sel, logits, NEG_BIG)
    m1 = jnp.max(el, axis=-1, keepdims=True)
    i1 = jnp.min(jnp.where(el == m1, lane, LANES), axis=-1, keepdims=True)
    el2 = jnp.where(lane == i1, NEG_BIG, el)
    m2 = jnp.max(el2, axis=-1, keepdims=True)
    i2 = jnp.min(jnp.where(el2 == m2, lane, LANES), axis=-1, keepdims=True)
    t = jnp.exp(m2 - m1)
    w1 = g_w / (1.0 + t)
    w2 = g_w * t / (1.0 + t)
    e1 = (i1 - N_GROUPS).astype(F32)
    e2 = (i2 - N_GROUPS).astype(F32)
    rt_ref[...] = jnp.where(lane == 0, e1,
                  jnp.where(lane == 1, e2,
                  jnp.where(lane == 2, w1,
                  jnp.where(lane == 3, w2, 0.0))))


def _router(h2d, g, wr, br, *, rows):
    n, d = h2d.shape
    return pl.pallas_call(
        _router_kernel,
        grid=(n // rows,),
        in_specs=[pl.BlockSpec((rows, d), lambda i: (i, 0)),
                  pl.BlockSpec((1, d), lambda i: (0, 0)),
                  pl.BlockSpec((d, 2 * LANES), lambda i: (0, 0)),
                  pl.BlockSpec((1, LANES), lambda i: (0, 0))],
        out_specs=[pl.BlockSpec((rows, d), lambda i: (i, 0)),
                   pl.BlockSpec((rows, LANES), lambda i: (i, 0))],
        out_shape=[jax.ShapeDtypeStruct((n, d), F32),
                   jax.ShapeDtypeStruct((n, LANES), F32)],
        compiler_params=_cparams(("parallel",)),
        name="router",
    )(h2d, g, wr, br)


def _expert_kernel(te_ref, nu_ref, tv_ref, first_ref, order_ref, u_ref, wg_ref, wu_ref,
                   wd_ref, y_ref, xbuf, ybuf, xb_ref, gsem, ssem, *, tm, n_tokens):
    i = pl.program_id(0)
    part = pl.program_id(1)
    n_used = nu_ref[0]

    def for_rows(n_valid, issue):
        groups = n_valid // ROW_DMA_UNROLL

        def group(g, carry):
            for j in range(ROW_DMA_UNROLL):
                issue(g * ROW_DMA_UNROLL + j)
            return carry

        def single(k, carry):
            issue(k)
            return carry

        lax.fori_loop(0, groups, group, 0)
        lax.fori_loop(groups * ROW_DMA_UNROLL, n_valid, single, 0)

    def gather(tile, slot, n_valid):
        base = first_ref[tile]

        def issue(k):
            e = order_ref[base + k]
            t = jnp.where(e >= n_tokens, e - n_tokens, e)
            pltpu.make_async_copy(u_ref.at[pl.ds(t, 1), :],
                                  xbuf.at[slot, pl.ds(k, 1), :], gsem.at[slot]).start()
        for_rows(n_valid, issue)

    def scatter(tile, slot, n_valid):
        base = first_ref[tile]

        def issue(k):
            r = order_ref[base + k]
            pltpu.make_async_copy(ybuf.at[slot, pl.ds(k, 1), :],
                                  y_ref.at[pl.ds(r, 1), :], ssem.at[slot]).start()
        for_rows(n_valid, issue)

    def rows_wait(copy_of, n_valid):
        bit = tm
        while bit >= 1:
            @pl.when((n_valid & bit) != 0)
            def _(bit=bit):
                copy_of(bit).wait()
            bit //= 2

    def gather_wait(slot, n_valid):
        rows_wait(lambda n: pltpu.make_async_copy(
            u_ref.at[pl.ds(0, n), :], xbuf.at[slot, pl.ds(0, n), :], gsem.at[slot]), n_valid)

    def scatter_wait(slot, n_valid):
        rows_wait(lambda n: pltpu.make_async_copy(
            ybuf.at[slot, pl.ds(0, n), :], y_ref.at[pl.ds(0, n), :], ssem.at[slot]), n_valid)

    @pl.when(i < n_used)
    def _():
        slot = i % 2

        @pl.when(part == 0)
        def _():
            @pl.when(i == 0)
            def _():
                xbuf[...] = jnp.zeros_like(xbuf)
                gather(0, 0, tv_ref[0])

            @pl.when(i + 1 < n_used)
            def _():
                gather(i + 1, 1 - slot, tv_ref[i + 1])

            gather_wait(slot, tv_ref[i])

            @pl.when(i >= 2)
            def _():
                scatter_wait(slot, tv_ref[i - 2])

            xb_ref[...] = xbuf[slot].astype(BF16)

        x = xb_ref[...]
        a = jnp.dot(x, wg_ref[0].astype(BF16), preferred_element_type=F32)
        b = jnp.dot(x, wu_ref[0].astype(BF16), preferred_element_type=F32)
        hid = (a / (1.0 + jnp.exp(-a)) * b).astype(BF16)
        y = jnp.dot(hid, wd_ref[0].astype(BF16), preferred_element_type=F32)

        @pl.when(part == 0)
        def _():
            ybuf[slot] = y

        @pl.when(part == 1)
        def _():
            ybuf[slot] = ybuf[slot] + y
            scatter(i, slot, tv_ref[i])

            @pl.when(i == n_used - 1)
            def _():
                scatter_wait(slot, tv_ref[i])

                @pl.when(i >= 1)
                def _():
                    scatter_wait(1 - slot, tv_ref[i - 1])


def _experts(tile_expert, n_used, tile_valid, first, order, u2, wg, wu, wd, *, tm):
    n_tiles = tile_expert.shape[0]
    n_tokens, d = u2.shape
    _, _, f = wg.shape
    kern = functools.partial(_expert_kernel, tm=tm, n_tokens=n_tokens)
    def half(i, p, nu):
        last = nu[0] - 1
        return jnp.where(i <= last, (i + p) % 2, (last + 1) % 2)
    fh = f // 2
    grid_spec = pltpu.PrefetchScalarGridSpec(
        num_scalar_prefetch=5,
        grid=(n_tiles, 2),
        in_specs=[pl.BlockSpec(memory_space=pl.ANY),
                  pl.BlockSpec((1, d, fh),
                               lambda i, p, te, nu, tv, fi, od: (te[i], 0, half(i, p, nu))),
                  pl.BlockSpec((1, d, fh),
                               lambda i, p, te, nu, tv, fi, od: (te[i], 0, half(i, p, nu))),
                  pl.BlockSpec((1, fh, d),
                               lambda i, p, te, nu, tv, fi, od: (te[i], half(i, p, nu), 0))],
        out_specs=pl.BlockSpec(memory_space=pl.ANY),
        scratch_shapes=[pltpu.VMEM((2, tm, d), F32), pltpu.VMEM((2, tm, d), F32),
                        pltpu.VMEM((tm, d), BF16),
                        pltpu.SemaphoreType.DMA((2,)), pltpu.SemaphoreType.DMA((2,))],
    )
    return pl.pallas_call(
        kern,
        grid_spec=grid_spec,
        out_shape=jax.ShapeDtypeStruct((order.shape[0], d), F32),
        compiler_params=_cparams(("arbitrary", "arbitrary")),
        name="experts",
    )(tile_expert, n_used, tile_valid, first, order, u2, wg, wu, wd)


def _combine_kernel(h_ref, y0_ref, y1_ref, rt_ref, g_ref, o_ref, *, final_norm):
    rt = rt_ref[...]
    w0 = rt[:, 2:3]
    w1 = rt[:, 3:4]
    h = h_ref[...] + w0 * y0_ref[...] + w1 * y1_ref[...]
    if final_norm:
        ms = jnp.mean(h * h, axis=-1, keepdims=True)
        h = h * lax.rsqrt(ms + RMS_EPS) * g_ref[...]
    o_ref[...] = h


def _combine(h2d, y, route, g, *, rows, final_norm):
    n, d = h2d.shape
    nb = n // rows
    return pl.pallas_call(
        functools.partial(_combine_kernel, final_norm=final_norm),
        grid=(nb,),
        in_specs=[pl.BlockSpec((rows, d), lambda i: (i, 0)),
                  pl.BlockSpec((rows, d), lambda i: (i, 0)),
                  pl.BlockSpec((rows, d), lambda i: (nb + i, 0)),
                  pl.BlockSpec((rows, LANES), lambda i: (i, 0)),
                  pl.BlockSpec((1, d), lambda i: (0, 0))],
        out_specs=pl.BlockSpec((rows, d), lambda i: (i, 0)),
        out_shape=jax.ShapeDtypeStruct((n, d), F32),
        compiler_params=_cparams(("parallel",)),
        name="combine",
    )(h2d, y, y, route, g)


def _dispatch_plan(eid, n_experts, tm, p_rows):
    assert TOP_K == 2
    n = eid.shape[0]
    n_flat = TOP_K * n
    n_tiles = p_rows // tm
    e_flat = eid.T.reshape(-1)
    onehot = (e_flat[:, None] == jnp.arange(n_experts, dtype=I32)[None, :]).astype(I32)
    counts = jnp.sum(onehot, axis=0)
    pcounts = ((counts + tm - 1) // tm) * tm
    pend = jnp.cumsum(pcounts)
    pstart = pend - pcounts
    start = jnp.cumsum(counts) - counts
    n_used = (pend[-1] // tm).astype(I32)
    assert n_experts * n_flat < 2 ** 31
    key = e_flat * n_flat + jnp.arange(n_flat, dtype=I32)
    neg_sorted, _ = lax.top_k(-key, n_flat)
    order = (-neg_sorted) % n_flat
    tile_start = jnp.arange(n_tiles, dtype=I32) * tm
    used_start = jnp.minimum(tile_start, pend[-1] - tm)
    tile_onehot = ((used_start[:, None] >= pstart[None, :])
                   & (used_start[:, None] < pend[None, :])).astype(I32)
    tile_expert = jnp.sum(tile_onehot * jnp.arange(n_experts, dtype=I32)[None, :], axis=1)
    valid_end = jnp.sum(tile_onehot * (pstart + counts)[None, :], axis=1)
    tile_valid = jnp.where(tile_start < pend[-1], jnp.clip(valid_end - tile_start, 0, tm), 0)
    first = jnp.sum(tile_onehot * (start - pstart)[None, :], axis=1) + used_start
    return (order.astype(I32), first.astype(I32), tile_expert, n_used.reshape(1),
            tile_valid.astype(I32))


def _tiles(n_tokens, d_model):
    return dict(
        norm_rows=256,
        wide_rows=512,
        in_tm=2048, in_tn=512,
        out_tm=1024, out_tn=1024,
        gla_tc=512,
        moe_tm=256,
    )


def kernel(x, positions, norm_mix_g, w_in, gla_gate_w2, gla_gate_b, gla_out_norm_g,
           swa_sinks, w_out, norm_ffn_g, router_group_w, router_group_b,
           router_expert_w, router_expert_b, expert_w_gate, expert_w_up,
           expert_w_down, norm_final_g):
    bsz, t, d = x.shape
    n = bsz * t
    depth = w_in.shape[0]
    cfg = _tiles(n, d)

    rank = gla_gate_w2.shape[1]
    key_w = gla_gate_w2.shape[2]
    val_w = GLA_HEADS * gla_out_norm_g.shape[1]
    dk, dv = key_w // GLA_HEADS, val_w // GLA_HEADS
    q_heads = swa_sinks.shape[1]
    kv_heads = q_heads // SWA_GROUP
    sq_w, skv_w = q_heads * SWA_HEAD_DIM, kv_heads * SWA_HEAD_DIM
    n_experts = expert_w_gate.shape[1]
    tm = cfg["moe_tm"]
    p_rows = TOP_K * n + n_experts * tm
    for rows in (cfg["in_tm"], cfg["out_tm"], cfg["wide_rows"], cfg["norm_rows"]):
        assert n % rows == 0, (n, rows)
    assert t % cfg["gla_tc"] == 0 and t % SWA_BLOCK == 0, t

    c_gq, c_gk, c_gv, c_gr = 0, key_w, 2 * key_w, 2 * key_w + val_w
    c_ga = 2 * key_w + 2 * val_w
    c_sq = c_ga + rank
    m_sq = c_ga
    m_sk, m_sv = m_sq + sq_w, m_sq + sq_w + skv_w

    half = SWA_HEAD_DIM // 2
    inv_freq = ROPE_THETA ** (-jnp.arange(half, dtype=F32) / half)
    freq_lane = jnp.tile(inv_freq, LANES // half).reshape(1, LANES)
    posf = jnp.broadcast_to(positions.astype(F32).reshape(n, 1), (n, LANES))
    cos_t, sin_t = _rope_tables(posf, freq_lane, rows=1024)

    h2d = x.reshape(n, d)
    for l in range(depth):
        w_bf = w_in[l].astype(BF16)
        w_swa = w_bf[:, c_sq:]
        w_ga = jnp.pad(w_bf[:, c_ga:c_sq], ((0, 0), (0, LANES - rank)))
        w2p = jnp.pad(gla_gate_w2[l], ((0, LANES - rank), (0, 0))).astype(BF16)

        u, ga = _norm_gate(h2d, norm_mix_g[l].reshape(1, d), w_ga, rows=cfg["wide_rows"])
        proj = _in_proj(u, w_bf, w_swa, c_ga, tm=cfg["in_tm"], tn=cfg["in_tn"])
        proj3 = proj.reshape(bsz, t, -1)

        o_gla = _gla(proj3, ga.reshape(bsz, t, LANES), w2p,
                     gla_gate_b[l].reshape(1, key_w), gla_out_norm_g[l].reshape(1, dv),
                     heads=GLA_HEADS, dk=dk, dv=dv, tc=cfg["gla_tc"],
                     col_q=c_gq, col_k=c_gk, col_v=c_gv, col_r=c_gr)
        sinks_lane = jnp.repeat(swa_sinks[l], SWA_HEAD_DIM).reshape(1, sq_w)
        o_swa = _swa(proj3, cos_t, sin_t, sinks_lane, q_heads=q_heads, kv_heads=kv_heads,
                     col_q=m_sq, col_k=m_sk, col_v=m_sv)

        h2d = _out_proj(o_gla.reshape(n, val_w), o_swa.reshape(n, sq_w),
                        w_out[l].astype(BF16), h2d, tm=cfg["out_tm"], tn=cfg["out_tn"])

        wr = jnp.concatenate(
            [router_group_w[l],
             jnp.transpose(router_expert_w[l], (1, 0, 2)).reshape(d, n_experts)], axis=1)
        n_router = wr.shape[1]
        wr = jnp.pad(wr, ((0, 0), (0, LANES - n_router)))
        wr_hi = wr.astype(BF16)
        wr = jnp.concatenate([wr_hi, (wr - wr_hi.astype(F32)).astype(BF16)], axis=1)
        br = jnp.pad(jnp.concatenate([router_group_b[l], router_expert_b[l].reshape(-1)]),
                     (0, LANES - n_router)).reshape(1, LANES)
        u2, route = _router(h2d, norm_ffn_g[l].reshape(1, d), wr, br, rows=cfg["wide_rows"])

        eid = route[:, :TOP_K].astype(I32)
        order, first, tile_expert, n_used, tile_valid = _dispatch_plan(
            eid, n_experts, tm, p_rows)
        y_tok = _experts(tile_expert, n_used, tile_valid, first, order, u2,
                         expert_w_gate[l], expert_w_up[l], expert_w_down[l], tm=tm)

        h2d = _combine(h2d, y_tok, route, norm_final_g.reshape(1, d),
                       rows=cfg["norm_rows"], final_norm=(l == depth - 1))
    return h2d.reshape(bsz, t, d)
```
